```python
import numpy as np
import jax
import jax.numpy as jnp
from jax import lax

D_MODEL = 1024
BATCH = 8
SEQ = 4096
DEPTH = 2

CTX_LEN = 256
GRID_W = 64

HEAD_DIM = 64
N_MIXERS = 4
GROUP_W = D_MODEL // N_MIXERS
MIX_HEADS = GROUP_W // HEAD_DIM
D_MIX = N_MIXERS * GROUP_W
D_FF = 2816
N_MOD = 9
ROPE_THETA = 10000.0
EPS = 1e-6
NEG_INF = -1e30

SWA_KV_HEADS = MIX_HEADS // 2
SWA_WINDOW = 128
SWA_BLOCK = 128
DN_CONV = 3
DN_CHUNK = 64
MLA_Q_RANK = D_MODEL // 4
MLA_KV_RANK = D_MODEL // 8
MLA_NOPE = 64
MLA_ROPE = 32
MLA_V = 64
MLA_BLOCK = 128
NA_KR = 8
NA_KC = 16
NA_QC = 16
NA_KSPAN = 2 * NA_QC

IN_SIZES = (GROUP_W, SWA_KV_HEADS * HEAD_DIM, SWA_KV_HEADS * HEAD_DIM,
            3 * GROUP_W, GROUP_W, 2 * MIX_HEADS, 2 * MIX_HEADS,
            MLA_Q_RANK, MLA_KV_RANK, MLA_ROPE,
            GROUP_W, GROUP_W, GROUP_W)
IN_PROJ = sum(IN_SIZES)

kernel_name = 'hybrid_parallel_heads_dit_block'

F32 = jnp.float32


def rms_norm(x, g):
    x32 = x.astype(F32)
    y = x32 * lax.rsqrt(jnp.mean(x32 * x32, axis=-1, keepdims=True) + EPS)
    return (y * g.astype(F32)).astype(x.dtype)


def l2_normalize(x):
    x32 = x.astype(F32)
    return x32 * lax.rsqrt(jnp.sum(x32 * x32, axis=-1, keepdims=True) + EPS)


def modulation(cvec, w, b):
    m = jax.nn.silu(cvec) @ w + b
    m = m.reshape(m.shape[0], 1, N_MOD, D_MODEL)
    return [m[:, :, j] for j in range(N_MOD)]


def pre_norm(h, g, shift, scale):
    return rms_norm(h, g) * (1.0 + scale) + shift


def swiglu(h, wg, wu, wd):
    return (jax.nn.silu(h @ wg) * (h @ wu)) @ wd


def rope_2d(x, row, col):
    half = x.shape[-1] // 2

    def rot(t, pos):
        n = t.shape[-1]
        inv = ROPE_THETA ** (-jnp.arange(0, n, 2, dtype=F32) / n)
        ang = pos[:, None] * inv[None, :]
        cos = jnp.cos(ang)[:, None, :]
        sin = jnp.sin(ang)[:, None, :]
        t1, t2 = jnp.split(t.astype(F32), 2, axis=-1)
        return jnp.concatenate([t1 * cos - t2 * sin, t2 * cos + t1 * sin], axis=-1)

    return jnp.concatenate([rot(x[..., :half], row), rot(x[..., half:], col)], axis=-1).astype(x.dtype)


def joint_softmax(*logits):
    sizes = [l.shape[-1] for l in logits]
    p = jax.nn.softmax(jnp.concatenate([l.astype(F32) for l in logits], axis=-1), axis=-1)
    return jnp.split(p, np.cumsum(sizes)[:-1].tolist(), axis=-1)


def context_attention(q, k, v):
    s = jnp.einsum('bqhd,bkhd->bhqk', q, k).astype(F32)
    p = jax.nn.softmax(s, axis=-1).astype(v.dtype)
    return jnp.einsum('bhqk,bkhd->bqhd', p, v)


def split_in(u):
    return jnp.split(u, np.cumsum(IN_SIZES)[:-1].tolist(), axis=-1)


def swa_mixer(q, k, v, qc, kc, vc, sink, row, col, need_ctx):
    B, S, _ = q.shape
    L = qc.shape[1]
    H, KH, d = MIX_HEADS, SWA_KV_HEADS, HEAD_DIM
    G = H // KH
    scale = d ** -0.5
    q = (rope_2d(q.reshape(B, S, H, d), row, col) * scale).reshape(B, S, KH, G, d)
    k = rope_2d(k.reshape(B, S, KH, d), row, col)
    v = v.reshape(B, S, KH, d)
    kc = kc.reshape(B, L, KH, d)
    vc = vc.reshape(B, L, KH, d)
    sink = sink.astype(F32).reshape(KH, G)
    nb = S // SWA_BLOCK
    span = SWA_BLOCK + 2 * SWA_WINDOW
    idx = np.arange(nb)[:, None] * SWA_BLOCK + np.arange(span)[None, :]
    qpos = np.arange(nb)[:, None] * SWA_BLOCK + np.arange(SWA_BLOCK)[None, :]
    kpos = idx - SWA_WINDOW
    valid = ((np.abs(qpos[:, :, None] - kpos[:, None, :]) <= SWA_WINDOW)
             & (kpos[:, None, :] >= 0) & (kpos[:, None, :] < S))
    pad = ((0, 0), (SWA_WINDOW, SWA_WINDOW), (0, 0), (0, 0))
    kb = jnp.pad(k, pad)[:, idx]
    vb = jnp.pad(v, pad)[:, idx]
    qb = q.reshape(B, nb, SWA_BLOCK, KH, G, d)
    s_loc = jnp.where(valid, jnp.einsum('bnqkgd,bnjkd->bkgnqj', qb, kb).astype(F32), NEG_INF)
    s_ctx = jnp.einsum('bnqkgd,bjkd->bkgnqj', qb, kc)
    s_sink = jnp.broadcast_to(sink[None, :, :, None, None, None], s_ctx.shape[:-1] + (1,))
    p_loc, p_ctx, _ = joint_softmax(s_loc, s_ctx, s_sink)
    o = (jnp.einsum('bkgnqj,bnjkd->bnqkgd', p_loc.astype(v.dtype), vb)
         + jnp.einsum('bkgnqj,bjkd->bnqkgd', p_ctx.astype(v.dtype), vc)).reshape(B, S, GROUP_W)
    if not need_ctx:
        return o, None
    qc = qc.reshape(B, L, KH, G, d) * scale
    sc = jnp.einsum('bqkgd,bjkd->bkgqj', qc, kc)
    sc_sink = jnp.broadcast_to(sink[None, :, :, None, None], sc.shape[:-1] + (1,))
    pc, _ = joint_softmax(sc, sc_sink)
    oc = jnp.einsum('bkgqj,bjkd->bqkgd', pc.astype(vc.dtype), vc).reshape(B, L, GROUP_W)
    return o, oc


def short_conv(x, w):
    C = x.shape[-1]
    p = DN_CONV // 2
    return lax.conv_general_dilated(x, w[:, None, :].astype(x.dtype), window_strides=(1,),
                                    padding=[(p, p)], dimension_numbers=('NWC', 'WIO', 'NWC'),
                                    feature_group_count=C)


def gated_delta_chunked(q, k, v, g, beta, state, with_out):
    B, T, H, dk = q.shape
    dv = v.shape[-1]
    C = DN_CHUNK
    N = T // C

    def blk(t):
        t = t.astype(F32).reshape((B, N, C, H) + t.shape[3:])
        return jnp.moveaxis(t, (1, 3), (0, 2))

    qb, kb, vb, gb, bb = blk(q), blk(k), blk(v), blk(g), blk(beta)
    gc = jnp.cumsum(gb, axis=-1)
    incl = np.tril(np.ones((C, C), dtype=bool))
    strict = np.tril(np.ones((C, C), dtype=bool), -1)
    diff = gc[..., :, None] - gc[..., None, :]
    decay = jnp.where(incl, jnp.exp(jnp.where(incl, diff, 0.0)), 0.0)
    kbeta = kb * bb[..., None]
    a_mat = jnp.where(strict, jnp.einsum('nbhcd,nbhjd->nbhcj', kbeta, kb) * decay, 0.0) + jnp.eye(C, dtype=F32)
    u = lax.linalg.triangular_solve(a_mat, vb * bb[..., None], left_side=True, lower=True, unit_diagonal=True)
    w = lax.linalg.triangular_solve(a_mat, kbeta * jnp.exp(gc)[..., None], left_side=True, lower=True,
                                    unit_diagonal=True)
    xs = (kb, u, w, gc)
    if with_out:
        intra = jnp.einsum('nbhcd,nbhjd->nbhcj', qb, kb) * decay
        xs = xs + (qb, intra)

    def step(s, xs_n):
        k_n, u_n, w_n, g_n = xs_n[:4]
        v_new = u_n - jnp.einsum('bhcd,bhde->bhce', w_n, s)
        g_last = g_n[..., -1]
        s_next = (s * jnp.exp(g_last)[..., None, None]
                  + jnp.einsum('bhcd,bhce->bhde', k_n * jnp.exp(g_last[..., None] - g_n)[..., None], v_new))
        if with_out:
            q_n, a_n = xs_n[4:]
            o = (jnp.einsum('bhcd,bhde->bhce', q_n * jnp.exp(g_n)[..., None], s)
                 + jnp.einsum('bhcj,bhje->bhce', a_n, v_new))
            return s_next, o
        return s_next, None

    s_fin, o = lax.scan(step, state.astype(F32), xs)
    if with_out:
        o = jnp.moveaxis(o, (0, 2), (1, 3)).reshape(B, T, H, dv)
    return o, s_fin


def deltanet_mixer(qkv, z, a, b, qkv_c, z_c, a_c, b_c, conv_w, a_log, dt_bias, norm_g, need_ctx):
    H, d = MIX_HEADS, HEAD_DIM

    def prep(qkv_t, a_t, b_t):
        B, T, _ = qkv_t.shape
        h = jax.nn.silu(short_conv(qkv_t, conv_w))
        q_t, k_t, v_t = jnp.split(h, 3, axis=-1)
        q_t = l2_normalize(q_t.reshape(B, T, H, d)) * (d ** -0.5)
        k_t = l2_normalize(k_t.reshape(B, T, H, d))
        v_t = v_t.reshape(B, T, H, d).astype(F32)
        g_t = -jnp.exp(a_log.astype(F32)) * jax.nn.softplus(a_t.reshape(B, T, 2, H).astype(F32)
                                                            + dt_bias.astype(F32))
        beta_t = jax.nn.sigmoid(b_t.reshape(B, T, 2, H).astype(F32))
        return q_t, k_t, v_t, g_t, beta_t

    def gated_out(o, z_t):
        B, T = z_t.shape[:2]
        y = rms_norm(o, norm_g) * jax.nn.silu(z_t.reshape(B, T, H, d).astype(F32))
        return y.reshape(B, T, GROUP_W).astype(z_t.dtype)

    def rev(t):
        return jnp.flip(t, axis=1)

    qc, kc, vc, gcx, bcx = prep(qkv_c, a_c, b_c)
    ql, kl, vl, gl, bl = prep(qkv, a, b)
    s0 = jnp.zeros((qkv.shape[0], H, d, d), F32)
    oc_f, s_f = gated_delta_chunked(qc, kc, vc, gcx[:, :, 0], bcx[:, :, 0], s0, need_ctx)
    oc_b, s_b = gated_delta_chunked(rev(qc), rev(kc), rev(vc), rev(gcx[:, :, 1]), rev(bcx[:, :, 1]), s0, need_ctx)
    ol_f, _ = gated_delta_chunked(ql, kl, vl, gl[:, :, 0], bl[:, :, 0], s_f, True)
    ol_b, _ = gated_delta_chunked(rev(ql), rev(kl), rev(vl), rev(gl[:, :, 1]), rev(bl[:, :, 1]), s_b, True)
    y = gated_out(ol_f + rev(ol_b), z)
    if not need_ctx:
        return y, None
    return y, gated_out(oc_f + rev(oc_b), z_c)


def mla_mixer(cq, ckv, kr, cq_c, ckv_c, kr_c, q_norm_g, w_uq, kv_norm_g, w_ukv, row, col, need_ctx):
    H = MIX_HEADS
    dqk = MLA_NOPE + MLA_ROPE
    scale = dqk ** -0.5

    def heads(cq_t, ckv_t, kr_t, rotary):
        B, T, _ = cq_t.shape
        q_t = (rms_norm(cq_t, q_norm_g) @ w_uq).reshape(B, T, H, dqk)
        kv = (rms_norm(ckv_t, kv_norm_g) @ w_ukv).reshape(B, T, H, MLA_NOPE + MLA_V)
        q_nope, q_rope = q_t[..., :MLA_NOPE], q_t[..., MLA_NOPE:]
        k_nope, v_t = kv[..., :MLA_NOPE], kv[..., MLA_NOPE:]
        k_rope = kr_t.reshape(B, T, 1, MLA_ROPE)
        if rotary:
            q_rope = rope_2d(q_rope, row, col)
            k_rope = rope_2d(k_rope, row, col)
        q_t = jnp.concatenate([q_nope, q_rope], axis=-1) * scale
        k_t = jnp.concatenate([k_nope, jnp.broadcast_to(k_rope, (B, T, H, MLA_ROPE))], axis=-1)
        return q_t, k_t, v_t

    q, k, v = heads(cq, ckv, kr, True)
    qc, kc, vc = heads(cq_c, ckv_c, kr_c, False)
    B, S = q.shape[:2]
    k_all = jnp.concatenate([kc, k], axis=1)
    v_all = jnp.concatenate([vc, v], axis=1)
    nb = S // MLA_BLOCK

    def block(qi):
        s = jnp.einsum('bqhd,bkhd->bhqk', qi, k_all).astype(F32)
        p = jax.nn.softmax(s, axis=-1).astype(v_all.dtype)
        return jnp.einsum('bhqk,bkhd->bqhd', p, v_all)

    qb = jnp.moveaxis(q.reshape(B, nb, MLA_BLOCK, H, dqk), 1, 0)
    o = jnp.moveaxis(lax.map(block, qb), 0, 1).reshape(B, S, H * MLA_V)
    if not need_ctx:
        return o, None
    oc = context_attention(qc, kc, vc).reshape(B, qc.shape[1], H * MLA_V)
    return o, oc


def na_mixer(q, k, v, qc, kc, vc, rpb, need_ctx):
    B, S, _ = q.shape
    L = qc.shape[1]
    H, d, W = MIX_HEADS, HEAD_DIM, GRID_W
    rows = S // W
    kr = min(NA_KR, rows)
    scale = d ** -0.5
    qg = q.reshape(B, rows, W, H, d) * scale
    kg = k.reshape(B, rows, W, H, d)
    vg = v.reshape(B, rows, W, H, d)
    kc = kc.reshape(B, L, H, d)
    vc = vc.reshape(B, L, H, d)
    ncb = W // NA_QC
    col_start = np.clip(np.arange(ncb) * NA_QC - NA_KC // 2, 0, W - NA_KSPAN)
    kcol = col_start[:, None] + np.arange(NA_KSPAN)[None, :]
    qcol = np.arange(ncb)[:, None] * NA_QC + np.arange(NA_QC)[None, :]
    cs = np.clip(qcol - NA_KC // 2, 0, W - NA_KC)
    col_ok = (kcol[:, None, :] >= cs[:, :, None]) & (kcol[:, None, :] < cs[:, :, None] + NA_KC)
    dc_idx = np.clip(kcol[:, None, :] - qcol[:, :, None] + NA_KC - 1, 0, 2 * NA_KC - 2)

    def row_block(args):
        r, q_row = args
        rs = jnp.clip(r - kr // 2, 0, rows - kr)
        k_rows = lax.dynamic_slice_in_dim(kg, rs, kr, axis=1)[:, :, kcol]
        v_rows = lax.dynamic_slice_in_dim(vg, rs, kr, axis=1)[:, :, kcol]
        qb = q_row.reshape(B, ncb, NA_QC, H, d)
        dr_idx = rs + jnp.arange(kr) - r + NA_KR - 1
        bias = rpb[:, dr_idx[:, None, None, None], dc_idx[None]]
        s_loc = (jnp.einsum('bnqhd,brnkhd->bhnqrk', qb, k_rows).astype(F32)
                 + jnp.transpose(bias, (0, 2, 3, 1, 4)).astype(F32))
        s_loc = jnp.where(col_ok[:, :, None, :], s_loc, NEG_INF).reshape(B, H, ncb, NA_QC, kr * NA_KSPAN)
        s_ctx = jnp.einsum('bnqhd,bkhd->bhnqk', qb, kc)
        p_loc, p_ctx = joint_softmax(s_loc, s_ctx)
        p_loc = p_loc.reshape(B, H, ncb, NA_QC, kr, NA_KSPAN).astype(v_rows.dtype)
        o = (jnp.einsum('bhnqrk,brnkhd->bnqhd', p_loc, v_rows)
             + jnp.einsum('bhnqk,bkhd->bnqhd', p_ctx.astype(vc.dtype), vc))
        return o.reshape(B, W, H * d)

    o = lax.map(row_block, (jnp.arange(rows), jnp.moveaxis(qg, 1, 0)))
    o = jnp.moveaxis(o, 0, 1).reshape(B, S, GROUP_W)
    if not need_ctx:
        return o, None
    oc = context_attention(qc.reshape(B, L, H, d) * scale, kc, vc).reshape(B, L, GROUP_W)
    return o, oc


def setup_inputs(seed: int = 0) -> dict:
    key = jax.random.key(seed)
    ks = iter(jax.random.split(key, 32))
    L, D = DEPTH, D_MODEL

    def nrm(shape, std):
        return std * jax.random.normal(next(ks), shape, F32)

    def gain(shape):
        return 1.0 + nrm(shape, 0.02)

    dt = jnp.exp(jax.random.uniform(next(ks), (L, 2, MIX_HEADS), F32, float(np.log(1e-3)), float(np.log(1e-1))))
    return {
        'x': nrm((BATCH, SEQ, D), 1.0),
        'c': nrm((BATCH, D), 1.0),
        'ctx': nrm((BATCH, CTX_LEN, D), 1.0),
        'c_ctx': nrm((D,), 1.0),
        'ada_w': nrm((L, D, N_MOD * D), 0.5 * D ** -0.5),
        'ada_b': nrm((L, N_MOD * D), 0.01),
        'norm1_g': gain((L, D)),
        'ffn1_wg': nrm((L, D, D_FF), D ** -0.5),
        'ffn1_wu': nrm((L, D, D_FF), D ** -0.5),
        'ffn1_wd': nrm((L, D_FF, D), D_FF ** -0.5),
        'norm2_g': gain((L, D)),
        'w_in': nrm((L, D, IN_PROJ), D ** -0.5),
        'swa_sink': nrm((L, MIX_HEADS), 0.5),
        'dn_conv_w': nrm((L, DN_CONV, 3 * GROUP_W), DN_CONV ** -0.5),
        'dn_a_log': jnp.log(jax.random.uniform(next(ks), (L, 2, MIX_HEADS), F32, 1.0, 16.0)),
        'dn_dt_bias': dt + jnp.log(-jnp.expm1(-dt)),
        'dn_norm_g': gain((L, HEAD_DIM)),
        'mla_q_norm_g': gain((L, MLA_Q_RANK)),
        'mla_w_uq': nrm((L, MLA_Q_RANK, MIX_HEADS * (MLA_NOPE + MLA_ROPE)), MLA_Q_RANK ** -0.5),
        'mla_kv_norm_g': gain((L, MLA_KV_RANK)),
        'mla_w_ukv': nrm((L, MLA_KV_RANK, MIX_HEADS * (MLA_NOPE + MLA_V)), MLA_KV_RANK ** -0.5),
        'na_rpb': nrm((L, MIX_HEADS, 2 * NA_KR - 1, 2 * NA_KC - 1), 0.1),
        'w_out': nrm((L, D_MIX, D), D_MIX ** -0.5),
        'norm3_g': gain((L, D)),
        'ffn2_wg': nrm((L, D, D_FF), D ** -0.5),
        'ffn2_wu': nrm((L, D, D_FF), D ** -0.5),
        'ffn2_wd': nrm((L, D_FF, D), D_FF ** -0.5),
        'final_norm_g': gain((D,)),
    }


def reference(x, c, ctx, c_ctx, ada_w, ada_b, norm1_g, ffn1_wg, ffn1_wu, ffn1_wd, norm2_g, w_in,
              swa_sink, dn_conv_w, dn_a_log, dn_dt_bias, dn_norm_g, mla_q_norm_g, mla_w_uq,
              mla_kv_norm_g, mla_w_ukv, na_rpb, w_out, norm3_g, ffn2_wg, ffn2_wu, ffn2_wd, final_norm_g):
    S = x.shape[1]
    t = jnp.arange(S)
    row = (t // GRID_W).astype(F32)
    col = (t % GRID_W).astype(F32)
    xc = ctx
    for i in range(DEPTH):
        need_ctx = i < DEPTH - 1
        mx = modulation(c, ada_w[i], ada_b[i])
        mc = modulation(c_ctx[None, :], ada_w[i], ada_b[i])
        x = x + 0.5 * mx[2] * swiglu(pre_norm(x, norm1_g[i], mx[0], mx[1]), ffn1_wg[i], ffn1_wu[i], ffn1_wd[i])
        xc = xc + 0.5 * mc[2] * swiglu(pre_norm(xc, norm1_g[i], mc[0], mc[1]), ffn1_wg[i], ffn1_wu[i], ffn1_wd[i])
        ux = split_in(pre_norm(x, norm2_g[i], mx[3], mx[4]) @ w_in[i])
        uc = split_in(pre_norm(xc, norm2_g[i], mc[3], mc[4]) @ w_in[i])
        ya, yac = swa_mixer(ux[0], ux[1], ux[2], uc[0], uc[1], uc[2], swa_sink[i], row, col, need_ctx)
        yd, ydc = deltanet_mixer(ux[3], ux[4], ux[5], ux[6], uc[3], uc[4], uc[5], uc[6], dn_conv_w[i],
                                 dn_a_log[i], dn_dt_bias[i], dn_norm_g[i], need_ctx)
        ym, ymc = mla_mixer(ux[7], ux[8], ux[9], uc[7], uc[8], uc[9], mla_q_norm_g[i], mla_w_uq[i],
                            mla_kv_norm_g[i], mla_w_ukv[i], row, col, need_ctx)
        yn, ync = na_mixer(ux[10], ux[11], ux[12], uc[10], uc[11], uc[12], na_rpb[i], need_ctx)
        x = x + mx[5] * (jnp.concatenate([ya, yd, ym, yn], axis=-1) @ w_out[i])
        x = x + 0.5 * mx[8] * swiglu(pre_norm(x, norm3_g[i], mx[6], mx[7]), ffn2_wg[i], ffn2_wu[i], ffn2_wd[i])
        if need_ctx:
            xc = xc + mc[5] * (jnp.concatenate([yac, ydc, ymc, ync], axis=-1) @ w_out[i])
            xc = xc + 0.5 * mc[8] * swiglu(pre_norm(xc, norm3_g[i], mc[6], mc[7]), ffn2_wg[i], ffn2_wu[i],
                                           ffn2_wd[i])
    return rms_norm(x, final_norm_g)
```

```python
import functools

import numpy as np
import jax
import jax.numpy as jnp
from jax import lax
from jax.experimental import pallas as pl
from jax.experimental.pallas import tpu as pltpu

F32 = jnp.float32
BF16 = jnp.bfloat16
HIGHEST = lax.Precision.HIGHEST

D_MODEL = 1024
CTX_LEN = 256
GRID_W = 64
HEAD_DIM = 64
N_HEADS = 4
GROUP_W = N_HEADS * HEAD_DIM
D_FF = 2816
N_MOD = 9
ROPE_THETA = 10000.0
EPS = 1e-6
NEG_INF = -1e30
SWA_WINDOW = 128
SWA_BLOCK = 128
DN_CHUNK = 64
DN_LEVEL0 = 5
MLA_Q_RANK = 256
MLA_KV_RANK = 128
MLA_NOPE = 64
MLA_ROPE = 32
MLA_V = 64
NA_KR = 8
NA_KC = 16
IN_SIZES = (256, 128, 128, 768, 256, 8, 8, 256, 128, 32, 256, 256, 256)
IN_PROJ = sum(IN_SIZES)

LANES = 128
TOKEN_TILE = 256
VMEM_LIMIT = 56 * 1024 * 1024
MOD_ROWS = 16
CTX_ROW = 8

P_SWA_Q, P_SWA_QS, P_SWA_K, P_SWA_KS, P_SWA_V = 0, 256, 512, 640, 768
P_DN_QKV, P_DN_Z, P_DN_AB = 896, 1664, 1920
P_MLA = 2048
P_NA_Q, P_NA_KV = 2688, 2944
P_TOTAL = 3456


def _cparams(sem):
    return pltpu.CompilerParams(dimension_semantics=sem, vmem_limit_bytes=VMEM_LIMIT)


def _dot(a, b, precision=None):
    return jnp.dot(a, b, preferred_element_type=F32, precision=precision)


def _dot_nt(a, b):
    return lax.dot_general(a, b, (((1,), (1,)), ((), ())), preferred_element_type=F32)


def _dot_tn(a, b):
    return lax.dot_general(a, b, (((0,), (0,)), ((), ())), preferred_element_type=F32)


def _sigmoid(x):
    return 1.0 / (1.0 + jnp.exp(-x))


def _silu(x):
    return x * _sigmoid(x)


def _resident(shape):
    nd = len(shape)
    return pl.BlockSpec(shape, lambda *_: (0,) * nd, pipeline_mode=pl.Buffered(1))


def _mod_kernel(c_ref, w_ref, b_ref, o_ref):
    s = _silu(c_ref[...])
    o_ref[0] = _dot(s, w_ref[0], HIGHEST) + b_ref[0]


def _modulation(cvec, ada_w, ada_b):
    n_layers = ada_w.shape[0]
    d = D_MODEL
    return pl.pallas_call(
        _mod_kernel,
        grid=(n_layers, N_MOD),
        in_specs=[
            pl.BlockSpec((MOD_ROWS, d), lambda l, j: (0, 0)),
            pl.BlockSpec((1, d, d), lambda l, j: (l, 0, j)),
            pl.BlockSpec((1, 1, d), lambda l, j: (l, 0, j)),
        ],
        out_specs=pl.BlockSpec((1, MOD_ROWS, d), lambda l, j: (l, 0, j)),
        out_shape=jax.ShapeDtypeStruct((n_layers, MOD_ROWS, N_MOD * d), F32),
        compiler_params=_cparams(("arbitrary", "arbitrary")),
        name="modulation",
    )(cvec, ada_w, ada_b.reshape(n_layers, 1, N_MOD * d))


def _mod_index(b, i):
    return (jnp.where(i == 0, CTX_ROW, b), 0, 0)


def _prenorm(x, g, shift, scale):
    ms = jnp.mean(x * x, axis=-1, keepdims=True)
    return (x * lax.rsqrt(ms + EPS) * g) * (1.0 + scale) + shift


def _group_mean_sq(x, ones_bd):
    return _dot(x * x, ones_bd, HIGHEST) * (1.0 / HEAD_DIM)


def _ffn_kernel(*refs, mix, final):
    it = iter(refs)
    x_ref = next(it)
    if mix:
        ya_ref, of_ref, ob_ref, z_ref, ym_ref, yn_ref = (next(it) for _ in range(6))
        mmod_ref, dng_ref, ones_ref, wo_ref = (next(it) for _ in range(4))
    mod_ref, g_ref, wg_ref, wu_ref, wd_ref = (next(it) for _ in range(5))
    if final:
        fg_ref = next(it)
    o_ref = next(it)

    x = x_ref[0]
    if mix:
        o = of_ref[0] + ob_ref[0]
        yd = (o * lax.rsqrt(_group_mean_sq(o, ones_ref[...]) + EPS) * dng_ref[...]) * _silu(z_ref[0])
        ycat = jnp.concatenate(
            [ya_ref[0].astype(BF16), yd.astype(BF16), ym_ref[0].astype(BF16), yn_ref[0].astype(BF16)], axis=-1)
        x = x + mmod_ref[0][2:3] * _dot(ycat, wo_ref[...])
    mod = mod_ref[0]
    h = _prenorm(x, g_ref[...], mod[0:1], mod[1:2]).astype(BF16)
    a = (_silu(_dot(h, wg_ref[...])) * _dot(h, wu_ref[...])).astype(BF16)
    x = x + (0.5 * mod[2:3]) * _dot(a, wd_ref[...])
    if final:
        ms = jnp.mean(x * x, axis=-1, keepdims=True)
        x = x * lax.rsqrt(ms + EPS) * fg_ref[...]
    o_ref[0] = x


def _ffn(x, mod, g, wg, wu, wd, mix=None, final_g=None, latent_only=False):
    bsz, n, d = x.shape
    tm = TOKEN_TILE
    off = 1 if latent_only else 0
    tiles = n // tm - off
    tok = lambda w: pl.BlockSpec((1, tm, w), lambda b, i: (b, i + off, 0))
    modspec = pl.BlockSpec((1, 3, d), lambda b, i: _mod_index(b, i + off))
    args, specs = [x], [tok(d)]
    if mix is not None:
        ya, of, ob, z, ym, yn, mmod, dng, ones_bd, wo = mix
        args += [ya, of, ob, z, ym, yn, mmod, dng, ones_bd, wo]
        specs += [tok(ya.shape[-1]), tok(of.shape[-1]), tok(ob.shape[-1]), tok(z.shape[-1]),
                  tok(ym.shape[-1]), tok(yn.shape[-1]), modspec,
                  _resident(dng.shape), _resident(ones_bd.shape), _resident(wo.shape)]
    args += [mod, g, wg, wu, wd]
    specs += [modspec, _resident(g.shape), _resident(wg.shape), _resident(wu.shape), _resident(wd.shape)]
    if final_g is not None:
        args.append(final_g)
        specs.append(_resident(final_g.shape))
    return pl.pallas_call(
        functools.partial(_ffn_kernel, mix=mix is not None, final=final_g is not None),
        grid=(bsz, tiles),
        in_specs=specs,
        out_specs=pl.BlockSpec((1, tm, d), lambda b, i: (b, i, 0)),
        out_shape=jax.ShapeDtypeStruct((bsz, tiles * tm, d), F32),
        compiler_params=_cparams(("parallel", "arbitrary")),
        name="ffn_mix" if mix is not None else "ffn",
    )(*args)


def _inproj_kernel(x_ref, mod_ref, g_ref, w_ref, cos_ref, sin_ref,
                   swaq_ref, swakv_ref, dnqkv_ref, dnz_ref, dnab_ref, mla_ref, naq_ref, nakv_ref):
    mod = mod_ref[0]
    h = _prenorm(x_ref[0], g_ref[...], mod[0:1], mod[1:2]).astype(BF16)
    u = _dot(h, w_ref[...])
    cos, sin = cos_ref[...], sin_ref[...]
    cos2 = jnp.concatenate([cos, cos], axis=-1)
    sin2 = jnp.concatenate([sin, sin], axis=-1)
    scale = HEAD_DIM ** -0.5
    swaq_ref[0] = (u[:, P_SWA_Q:P_SWA_Q + 256] * cos2 + u[:, P_SWA_QS:P_SWA_QS + 256] * sin2) * scale
    k = u[:, P_SWA_K:P_SWA_K + 128] * cos + u[:, P_SWA_KS:P_SWA_KS + 128] * sin
    swakv_ref[0] = jnp.concatenate([k, u[:, P_SWA_V:P_SWA_V + 128]], axis=-1)
    dnqkv_ref[0] = u[:, P_DN_QKV:P_DN_QKV + 768]
    dnz_ref[0] = u[:, P_DN_Z:P_DN_Z + 256]
    dnab_ref[0] = u[:, P_DN_AB:P_DN_AB + 128]
    mla_ref[0] = u[:, P_MLA:P_MLA + 640]
    naq_ref[0] = u[:, P_NA_Q:P_NA_Q + 256] * scale
    nakv_ref[0] = u[:, P_NA_KV:P_NA_KV + 512]


def _inproj(x, mod, g, w, cos, sin):
    bsz, n, d = x.shape
    tm = TOKEN_TILE
    widths = (256, 256, 768, 256, 128, 640, 256, 512)
    tok = lambda w_: pl.BlockSpec((1, tm, w_), lambda b, i: (b, i, 0))
    tab = pl.BlockSpec((tm, LANES), lambda b, i: (i, 0))
    return pl.pallas_call(
        _inproj_kernel,
        grid=(bsz, n // tm),
        in_specs=[tok(d), pl.BlockSpec((1, 3, d), _mod_index), _resident(g.shape), _resident(w.shape), tab, tab],
        out_specs=[tok(w_) for w_ in widths],
        out_shape=[jax.ShapeDtypeStruct((bsz, n, w_), F32) for w_ in widths],
        compiler_params=_cparams(("parallel", "arbitrary")),
        name="inproj",
    )(x, mod, g, w, cos, sin)


def _swa_kernel(sink_ref, q_ref, kvp_ref, kvc_ref, kvn_ref, kvx_ref, band_ref, o_ref, *, first, last):
    j = pl.program_id(1)
    blk = SWA_BLOCK
    q = q_ref[0]
    lane = lax.broadcasted_iota(jnp.int32, (1, LANES), 1)
    hmask = [(lane < HEAD_DIM).astype(F32), (lane >= HEAD_DIM).astype(F32)]
    qs = jnp.concatenate([q[:, g * LANES:(g + 1) * LANES] * hmask[kh] for g in range(2) for kh in range(2)],
                         axis=0).astype(BF16)
    kvs = [kvp_ref[0], kvc_ref[0], kvn_ref[0], kvx_ref[0]]
    kcat = jnp.concatenate([t[:, :LANES] for t in kvs], axis=0).astype(BF16)
    vcat = jnp.concatenate([t[:, LANES:] for t in kvs], axis=0).astype(BF16)
    s = _dot_nt(qs, kcat) + band_ref[...]
    pen = [jnp.where(j > first, 0.0, NEG_INF), jnp.where(j >= first, 0.0, NEG_INF),
           jnp.where((j >= first) & (j < last), 0.0, NEG_INF)]
    zero = jnp.zeros((1, LANES), F32)
    s = s + jnp.concatenate([zero + pen[0], zero + pen[1], zero + pen[2], zero, zero], axis=-1)
    rblk = lax.broadcasted_iota(jnp.int32, (4 * blk, 1), 0) // blk
    sk = jnp.where(rblk == 0, sink_ref[0], jnp.where(rblk == 1, sink_ref[2],
                                                       jnp.where(rblk == 2, sink_ref[1], sink_ref[3])))
    m = jnp.maximum(jnp.max(s, axis=-1, keepdims=True), sk)
    p = jnp.exp(s - m)
    l = jnp.sum(p, axis=-1, keepdims=True) + jnp.exp(sk - m)
    o = _dot(p.astype(BF16), vcat) * (1.0 / l)
    outs = [o[(2 * g) * blk:(2 * g + 1) * blk] * hmask[0] + o[(2 * g + 1) * blk:(2 * g + 2) * blk] * hmask[1]
            for g in range(2)]
    o_ref[0] = jnp.concatenate(outs, axis=-1)


def _swa(q, kv, sink, band):
    bsz, n, _ = q.shape
    blk = SWA_BLOCK
    first = CTX_LEN // blk
    last = n // blk - 1
    clampi = lambda j: jnp.clip(j, first, last)
    tok = lambda f: pl.BlockSpec((1, blk, 256), f)
    return pl.pallas_call(
        functools.partial(_swa_kernel, first=first, last=last),
        grid=(bsz, n // blk),
        in_specs=[
            pl.BlockSpec(memory_space=pltpu.SMEM),
            tok(lambda b, j: (b, j, 0)),
            tok(lambda b, j: (b, clampi(j - 1), 0)),
            tok(lambda b, j: (b, clampi(j), 0)),
            tok(lambda b, j: (b, clampi(j + 1), 0)),
            pl.BlockSpec((1, CTX_LEN, 256), lambda b, j: (b, 0, 0)),
            _resident(band.shape),
        ],
        out_specs=tok(lambda b, j: (b, j, 0)),
        out_shape=jax.ShapeDtypeStruct((bsz, n, 256), F32),
        compiler_params=_cparams(("parallel", "arbitrary")),
        name="swa",
    )(sink, q, kv, kv, kv, kv, band)


def _na_kernel(q_ref, kv_ref, tab_ref, o_ref, *, rows):
    j = pl.program_id(1)
    w = GRID_W
    nctx = CTX_LEN // w
    r = jnp.maximum(j - nctx, 0)
    rs = jnp.clip(r - NA_KR // 2, 0, rows - NA_KR)
    tab_i = jnp.where(j < nctx, NA_KR, r - rs)
    q = q_ref[0]
    lane = lax.broadcasted_iota(jnp.int32, (1, GROUP_W), 1) // HEAD_DIM
    hmask = [(lane == h).astype(F32) for h in range(N_HEADS)]
    qs = jnp.concatenate([q * hmask[h] for h in range(N_HEADS)], axis=0).astype(BF16)
    start = pl.multiple_of(CTX_LEN + rs * w, w)
    kv_loc = kv_ref[0, pl.ds(start, NA_KR * w), :]
    kv_ctx = kv_ref[0, pl.ds(0, CTX_LEN), :]
    kcat = jnp.concatenate([kv_loc[:, :GROUP_W], kv_ctx[:, :GROUP_W]], axis=0).astype(BF16)
    vcat = jnp.concatenate([kv_loc[:, GROUP_W:], kv_ctx[:, GROUP_W:]], axis=0).astype(BF16)
    s = _dot_nt(qs, kcat)
    nloc = NA_KR * w
    s = jnp.concatenate([s[:, :nloc] + tab_ref[tab_i], s[:, nloc:]], axis=-1)
    m = jnp.max(s, axis=-1, keepdims=True)
    p = jnp.exp(s - m)
    l = jnp.sum(p, axis=-1, keepdims=True)
    o = _dot(p.astype(BF16), vcat) * (1.0 / l)
    acc = o[0:w] * hmask[0]
    for h in range(1, N_HEADS):
        acc = acc + o[h * w:(h + 1) * w] * hmask[h]
    o_ref[0] = acc


def _na(q, kv, tab):
    bsz, n, _ = q.shape
    w = GRID_W
    rows = (n - CTX_LEN) // w
    assert rows >= NA_KR
    return pl.pallas_call(
        functools.partial(_na_kernel, rows=rows),
        grid=(bsz, n // w),
        in_specs=[
            pl.BlockSpec((1, w, 256), lambda b, j: (b, j, 0)),
            pl.BlockSpec((1, n, 512), lambda b, j: (b, 0, 0)),
            _resident(tab.shape),
        ],
        out_specs=pl.BlockSpec((1, w, 256), lambda b, j: (b, j, 0)),
        out_shape=jax.ShapeDtypeStruct((bsz, n, 256), F32),
        compiler_params=_cparams(("parallel", "arbitrary")),
        name="na",
    )(q, kv, tab)


def _mla_prep_kernel(c_ref, gq_ref, gkv_ref, wq_ref, wqs_ref, wk_ref, wv_ref, cos_ref, sin_ref,
                     q_ref, k_ref, v_ref):
    c = c_ref[0]
    cq, ckv = c[:, 0:256], c[:, 256:384]
    kra, krb = c[:, 384:512], c[:, 512:640]
    nq = (cq * lax.rsqrt(jnp.mean(cq * cq, axis=-1, keepdims=True) + EPS) * gq_ref[...]).astype(BF16)
    nkv = (ckv * lax.rsqrt(jnp.mean(ckv * ckv, axis=-1, keepdims=True) + EPS) * gkv_ref[...]).astype(BF16)
    cos, sin = cos_ref[...], sin_ref[...]
    cos4 = jnp.concatenate([cos] * N_HEADS, axis=-1)
    sin4 = jnp.concatenate([sin] * N_HEADS, axis=-1)
    scale = (MLA_NOPE + MLA_ROPE) ** -0.5
    q_ref[0] = ((_dot(nq, wq_ref[...]) * cos4 + _dot(nq, wqs_ref[...]) * sin4) * scale).astype(BF16)
    kr = kra * cos + krb * sin
    k_ref[0] = (_dot(nkv, wk_ref[...]) + jnp.concatenate([kr] * N_HEADS, axis=-1)).astype(BF16)
    v_ref[0] = _dot(nkv, wv_ref[...]).astype(BF16)


def _mla_prep(c, gq, gkv, wq, wqs, wk, wv, cos, sin):
    bsz, n, _ = c.shape
    tm = TOKEN_TILE
    tok = lambda w_: pl.BlockSpec((1, tm, w_), lambda b, i: (b, i, 0))
    tab = pl.BlockSpec((tm, LANES), lambda b, i: (i, 0))
    return pl.pallas_call(
        _mla_prep_kernel,
        grid=(bsz, n // tm),
        in_specs=[tok(640)] + [_resident(a.shape) for a in (gq, gkv, wq, wqs, wk, wv)] + [tab, tab],
        out_specs=[tok(512)] * 3,
        out_shape=[jax.ShapeDtypeStruct((bsz, n, 512), BF16)] * 3,
        compiler_params=_cparams(("parallel", "arbitrary")),
        name="mla_prep",
    )(c, gq, gkv, wq, wqs, wk, wv, cos, sin)


def _mla_attn_kernel(q_ref, k_ref, v_ref, o_ref):
    i = pl.program_id(2)

    def attend(nkeys):
        q = q_ref[0]
        k = k_ref[0, pl.ds(0, nkeys), :]
        v = v_ref[0, pl.ds(0, nkeys), :]
        s = _dot_nt(q, k)
        m = jnp.max(s, axis=-1, keepdims=True)
        p = jnp.exp(s - m)
        l = jnp.sum(p, axis=-1, keepdims=True)
        o_ref[0] = _dot(p.astype(BF16), v) * (1.0 / l)

    @pl.when(i == 0)
    def _():
        attend(CTX_LEN)

    @pl.when(i > 0)
    def _():
        attend(k_ref.shape[1])


def _mla_attn(q, k, v):
    bsz, n, _ = q.shape
    tq = TOKEN_TILE
    return pl.pallas_call(
        _mla_attn_kernel,
        grid=(bsz, N_HEADS, n // tq),
        in_specs=[
            pl.BlockSpec((1, tq, LANES), lambda b, h, i: (b, i, h)),
            pl.BlockSpec((1, n, LANES), lambda b, h, i: (b, 0, h)),
            pl.BlockSpec((1, n, LANES), lambda b, h, i: (b, 0, h)),
        ],
        out_specs=pl.BlockSpec((1, tq, LANES), lambda b, h, i: (b, i, h)),
        out_shape=jax.ShapeDtypeStruct((bsz, n, N_HEADS * LANES), F32),
        compiler_params=_cparams(("parallel", "arbitrary", "arbitrary")),
        name="mla_attn",
    )(q, k, v)


def _bd(m, bd_mask):
    return jnp.concatenate([m] * N_HEADS, axis=0) * bd_mask


def _dn_local_kernel(x_ref, xp_ref, xn_ref, ab_ref, cw_ref, alog_ref, dtb_ref, ones_ref, bdm_ref,
                     exp_ref, tri_ref, stm_ref,
                     u_ref, w_ref, qg_ref, kd_ref, in_ref, egl_ref):
    i = pl.program_id(1)
    nt = pl.num_programs(1)
    tm = TOKEN_TILE
    c = DN_CHUNK
    x = x_ref[0]
    row = lax.broadcasted_iota(jnp.int32, (tm, 1), 0)
    has_prev = (i > 1).astype(F32)
    has_next = ((i > 0) & (i < nt - 1)).astype(F32)
    prev_row = xp_ref[0][7:8] * has_prev
    next_row = xn_ref[0][0:1] * has_next
    xm1 = jnp.where(row == 0, prev_row, pltpu.roll(x, 1, 0))
    xp1 = jnp.where(row == tm - 1, next_row, pltpu.roll(x, tm - 1, 0))
    cw = cw_ref[...]
    h = _silu(cw[0:1] * xm1 + cw[1:2] * x + cw[2:3] * xp1)
    q, k, v = h[:, 0:256], h[:, 256:512], h[:, 512:768]
    ones_bd = ones_ref[...]
    q = q * lax.rsqrt(_dot(q * q, ones_bd, HIGHEST) + EPS) * (HEAD_DIM ** -0.5)
    k = k * lax.rsqrt(_dot(k * k, ones_bd, HIGHEST) + EPS)
    ab = ab_ref[0]
    z = ab + dtb_ref[...]
    softplus = jnp.maximum(z, 0.0) + jnp.log(1.0 + jnp.exp(-jnp.abs(z)))
    g_all = -jnp.exp(alog_ref[...]) * softplus
    beta_all = _sigmoid(ab)
    bdm = bdm_ref[...]
    eye = stm_ref[4]
    tile_b = eye.astype(BF16)
    for d in range(2):
        gexp = _dot(g_all, exp_ref[d], HIGHEST)
        bexp = _dot(beta_all, exp_ref[2 + d], HIGHEST)
        incl, strict = stm_ref[2 * d], stm_ref[2 * d + 1]
        incl_t = stm_ref[2 - 2 * d]
        for ci in range(tm // c):
            sl = slice(ci * c, (ci + 1) * c)
            ge, be, kc, qc, vc = gexp[sl], bexp[sl], k[sl], q[sl], v[sl]
            gcol = _dot(tri_ref[d], ge, HIGHEST)
            grow = jnp.sum(ge * incl_t, axis=0, keepdims=True)
            dec_incl = jnp.exp((gcol - grow) * incl) * incl
            dec_strict = dec_incl * strict
            kb = kc * be
            kt_bd = (_dot_tn(kc.astype(BF16), tile_b).astype(BF16)) * bdm
            kq = _dot(jnp.concatenate([kb, qc], axis=0).astype(BF16), kt_bd)
            lmat = kq[:c] * dec_strict
            intra = kq[c:] * dec_incl
            lb = lmat.astype(BF16)
            xinv = eye - lmat * stm_ref[DN_LEVEL0 + 6 * d]
            for lvl in range(1, 6):
                off = _bd(lb * stm_ref[DN_LEVEL0 + 6 * d + lvl].astype(BF16), bdm)
                y = _dot(xinv.astype(BF16), off)
                xinv = xinv - _dot(y.astype(BF16), _bd(xinv.astype(BF16), bdm))
            egc = jnp.exp(gcol)
            xb = xinv.astype(BF16)
            u = _dot(xb, _bd((vc * be).astype(BF16), bdm))
            w = _dot(xb, _bd((kb * egc).astype(BF16), bdm))
            glast = gcol[c - 1:c] if d == 0 else gcol[0:1]
            u_ref[d, 0, sl, :] = u
            w_ref[d, 0, sl, :] = w.astype(BF16)
            qg_ref[d, 0, sl, :] = (qc * egc).astype(BF16)
            kd_ref[d, 0, sl, :] = (kc * jnp.exp(glast - gcol)).astype(BF16)
            in_ref[d, 0, sl, :] = intra.astype(BF16)
            egl_ref[d, 0, ci * 8:(ci + 1) * 8, :] = jnp.broadcast_to(jnp.exp(glast), (8, GROUP_W))


def _dn_local(qkv, ab, cw, alog, dtb, ones_bd, bdm, expand, tri, stm):
    bsz, n, _ = qkv.shape
    tm = TOKEN_TILE
    nt = n // tm
    r8 = tm // 8
    tok = lambda w_: pl.BlockSpec((1, tm, w_), lambda b, i: (b, i, 0))
    dtok = lambda: pl.BlockSpec((2, 1, tm, 256), lambda b, i: (0, b, i, 0))
    outs = [jax.ShapeDtypeStruct((2, bsz, n, 256), dt) for dt in (F32, BF16, BF16, BF16, BF16)]
    outs.append(jax.ShapeDtypeStruct((2, bsz, n // 8, 256), F32))
    return pl.pallas_call(
        _dn_local_kernel,
        grid=(bsz, nt),
        in_specs=[
            tok(768),
            pl.BlockSpec((1, 8, 768), lambda b, i: (b, jnp.maximum(i * r8 - 1, 0), 0)),
            pl.BlockSpec((1, 8, 768), lambda b, i: (b, jnp.minimum((i + 1) * r8, nt * r8 - 1), 0)),
            tok(128),
        ] + [_resident(a.shape) for a in (cw, alog, dtb, ones_bd, bdm, expand, tri, stm)],
        out_specs=[dtok() for _ in range(5)] + [pl.BlockSpec((2, 1, tm // 8, 256), lambda b, i: (0, b, i, 0))],
        out_shape=outs,
        compiler_params=_cparams(("parallel", "arbitrary")),
        name="dn_local",
    )(qkv, qkv, qkv, ab, cw, alog, dtb, ones_bd, bdm, expand, tri, stm)


def _dn_scan_kernel(uf_ref, wf_ref, qf_ref, kf_ref, if_ref, ef_ref,
                    ub_ref, wb_ref, qb_ref, kb_ref, ib_ref, eb_ref, bdm_ref,
                    of_ref, ob_ref, sf_ref, sb_ref):
    s_idx = pl.program_id(1)
    c = DN_CHUNK
    nch = TOKEN_TILE // c

    @pl.when(s_idx == 0)
    def _():
        sf_ref[...] = jnp.zeros_like(sf_ref)
        sb_ref[...] = jnp.zeros_like(sb_ref)

    bdm = bdm_ref[...]
    bdm_f = bdm.astype(F32)

    def chunk(refs, o_ref, s_ref, ci):
        u_ref, w_ref, q_ref, k_ref, i_ref, e_ref = refs
        sl = slice(ci * c, (ci + 1) * c)
        state = s_ref[...]
        ws = _dot(jnp.concatenate([w_ref[0, 0, sl, :], q_ref[0, 0, sl, :]], axis=0), state.astype(BF16))
        vnew = u_ref[0, 0, sl, :] - ws[:c]
        vb = vnew.astype(BF16)
        o_ref[0, sl, :] = ws[c:] + _dot(i_ref[0, 0, sl, :], _bd(vb, bdm))
        egl = e_ref[0, 0, ci * 8:ci * 8 + 1, :]
        s_ref[...] = state * egl + _dot_tn(k_ref[0, 0, sl, :], vb) * bdm_f

    fwd = (uf_ref, wf_ref, qf_ref, kf_ref, if_ref, ef_ref)
    bwd = (ub_ref, wb_ref, qb_ref, kb_ref, ib_ref, eb_ref)
    for ci in range(nch):
        chunk(fwd, of_ref, sf_ref, ci)
        chunk(bwd, ob_ref, sb_ref, nch - 1 - ci)


def _dn_scan(u, w, qg, kd, intra, egl, bdm):
    _, bsz, n, _ = u.shape
    tm = TOKEN_TILE
    nt = n // tm
    bidx = lambda s: jnp.where(s == 0, 0, nt - s)
    f4 = lambda w_: pl.BlockSpec((1, 1, w_, 256), lambda b, s: (0, b, s, 0))
    b4 = lambda w_: pl.BlockSpec((1, 1, w_, 256), lambda b, s: (1, b, bidx(s), 0))
    arrs = (u, w, qg, kd, intra)
    return pl.pallas_call(
        _dn_scan_kernel,
        grid=(bsz, nt),
        in_specs=[f4(tm)] * 5 + [f4(tm // 8)] + [b4(tm)] * 5 + [b4(tm // 8)] + [_resident(bdm.shape)],
        out_specs=[pl.BlockSpec((1, tm, 256), lambda b, s: (b, s, 0)),
                   pl.BlockSpec((1, tm, 256), lambda b, s: (b, bidx(s), 0))],
        out_shape=[jax.ShapeDtypeStruct((bsz, n, 256), F32)] * 2,
        scratch_shapes=[pltpu.VMEM((GROUP_W, GROUP_W), F32)] * 2,
        compiler_params=_cparams(("parallel", "arbitrary")),
        name="dn_scan",
    )(*arrs, egl, *arrs, egl, bdm)


def _half_swap(n):
    q = n // 4
    i = np.arange(n)
    return np.where((i % (2 * q)) < q, i + q, i - q)


def _inproj_columns():
    off = np.cumsum((0,) + IN_SIZES)
    z = IN_PROJ
    sw64, sw32 = _half_swap(64), _half_swap(32)
    gg, kk, dd = np.meshgrid(np.arange(2), np.arange(2), np.arange(64), indexing="ij")
    swa_q = (off[0] + kk * 128 + gg * 64 + dd).reshape(-1)
    swa_qs = (off[0] + kk * 128 + gg * 64 + sw64[dd]).reshape(-1)
    kk2, dd2 = np.meshgrid(np.arange(2), np.arange(64), indexing="ij")
    swa_k = (off[1] + kk2 * 64 + dd2).reshape(-1)
    swa_ks = (off[1] + kk2 * 64 + sw64[dd2]).reshape(-1)
    zeros = lambda n: np.full((n,), z)
    cols = np.concatenate([
        swa_q, swa_qs, swa_k, swa_ks, off[2] + np.arange(128),
        off[3] + np.arange(768), off[4] + np.arange(256),
        off[5] + np.arange(8), off[6] + np.arange(8), zeros(112),
        off[7] + np.arange(256), off[8] + np.arange(128),
        zeros(64), off[9] + np.arange(32), zeros(32),
        zeros(64), off[9] + sw32, zeros(32),
        off[10] + np.arange(256), off[11] + np.arange(256), off[12] + np.arange(256),
    ])
    assert cols.shape == (P_TOTAL,)
    return cols


def _gather_cols(w, cols):
    wz = jnp.concatenate([w, jnp.zeros(w.shape[:-1] + (1,), w.dtype)], axis=-1)
    return jnp.take(wz, jnp.asarray(cols), axis=-1)


def _mla_columns():
    dqk = MLA_NOPE + MLA_ROPE
    zq, zkv = N_HEADS * dqk, N_HEADS * (MLA_NOPE + MLA_V)
    sw32 = _half_swap(32)
    wq, wqs, wk, wv = [], [], [], []
    for h in range(N_HEADS):
        wq += [h * dqk + np.arange(dqk), np.full((32,), zq)]
        wqs += [np.full((64,), zq), h * dqk + MLA_NOPE + sw32, np.full((32,), zq)]
        wk += [h * 128 + np.arange(64), np.full((64,), zkv)]
        wv += [h * 128 + 64 + np.arange(64), np.full((64,), zkv)]
    return tuple(np.concatenate(c) for c in (wq, wqs, wk, wv))


def _wout_rows():
    z = 4 * GROUP_W
    gg, kk, dd = np.meshgrid(np.arange(2), np.arange(2), np.arange(64), indexing="ij")
    ya = (kk * 128 + gg * 64 + dd).reshape(-1)
    ym = np.concatenate([np.concatenate([2 * GROUP_W + h * 64 + np.arange(64), np.full((64,), z)])
                         for h in range(N_HEADS)])
    return np.concatenate([ya, GROUP_W + np.arange(256), ym, 3 * GROUP_W + np.arange(256)])


def _rope_tables(seq):
    t = jnp.arange(seq)
    row = (t // GRID_W).astype(F32)
    col = (t % GRID_W).astype(F32)

    def cs(n):
        inv = ROPE_THETA ** (-jnp.arange(0, n, 2, dtype=F32) / n)
        ar, ac = row[:, None] * inv[None, :], col[:, None] * inv[None, :]
        cos = jnp.concatenate([jnp.cos(ar)] * 2 + [jnp.cos(ac)] * 2, axis=-1)
        sin = jnp.concatenate([-jnp.sin(ar), jnp.sin(ar), -jnp.sin(ac), jnp.sin(ac)], axis=-1)
        return cos, sin

    cos64, sin64 = cs(HEAD_DIM // 2)
    cos32, sin32 = cs(MLA_ROPE // 2)
    ones = lambda w_: jnp.ones((CTX_LEN, w_), F32)
    zeros = lambda w_: jnp.zeros((CTX_LEN, w_), F32)
    cos_a = jnp.concatenate([ones(128), jnp.concatenate([cos64, cos64], axis=-1)], axis=0)
    sin_a = jnp.concatenate([zeros(128), jnp.concatenate([sin64, sin64], axis=-1)], axis=0)
    pad1, pad0 = jnp.ones((seq, 64), F32), jnp.zeros((seq, 32), F32)
    cos_m = jnp.concatenate([jnp.concatenate([ones(96), zeros(32)], axis=-1),
                             jnp.concatenate([pad1, cos32, pad0], axis=-1)], axis=0)
    sin_m = jnp.concatenate([zeros(128),
                             jnp.concatenate([jnp.zeros((seq, 64), F32), sin32, pad0], axis=-1)], axis=0)
    return cos_a, sin_a, cos_m, sin_m


def _swa_band():
    i = np.arange(4 * SWA_BLOCK)[:, None] % SWA_BLOCK
    j = np.arange(3 * SWA_BLOCK + CTX_LEN)[None, :]
    ok = (j >= 3 * SWA_BLOCK) | ((j >= i) & (j <= i + 2 * SWA_WINDOW))
    return jnp.asarray(np.where(ok, 0.0, NEG_INF), F32)


def _na_bias_table(rpb):
    w = GRID_W
    delta = np.arange(NA_KR)[:, None, None, None]
    qc = np.arange(w)[None, :, None, None]
    rr = np.arange(NA_KR)[None, None, :, None]
    kc = np.arange(w)[None, None, None, :]
    cs = np.clip(qc - NA_KC // 2, 0, w - NA_KC)
    ok = np.broadcast_to((kc >= cs) & (kc < cs + NA_KC), (NA_KR, w, NA_KR, w))
    dr = np.broadcast_to(rr - delta + NA_KR - 1, ok.shape)
    dc = np.broadcast_to(np.clip(kc - qc + NA_KC - 1, 0, 2 * NA_KC - 2), ok.shape)
    vals = rpb[:, jnp.asarray(dr), jnp.asarray(dc)]
    tab = jnp.where(jnp.asarray(ok)[None], vals, NEG_INF)
    tab = jnp.transpose(tab, (1, 0, 2, 3, 4)).reshape(NA_KR, N_HEADS * w, NA_KR * w)
    return jnp.concatenate([tab, jnp.full((1,) + tab.shape[1:], NEG_INF, F32)], axis=0)


def _dn_constants():
    c = DN_CHUNK
    hh = np.arange(GROUP_W) // HEAD_DIM
    bd = (hh[:, None] == hh[None, :])
    expand = np.zeros((4, LANES, GROUP_W), np.float32)
    for d in range(2):
        expand[d, d * 4 + hh, np.arange(GROUP_W)] = 1.0
        expand[2 + d, 8 + d * 4 + hh, np.arange(GROUP_W)] = 1.0
    i = np.arange(c)[:, None]
    j = np.arange(c)[None, :]
    tri = np.stack([(j <= i), (j >= i)]).astype(np.float32)
    jj = (np.arange(GROUP_W) % c)[None, :]
    masks = [(jj <= i), (jj < i), (jj >= i), (jj > i), (jj == i)]
    for d in range(2):
        for lvl in range(6):
            b = 1 << lvl
            same = (i // (2 * b)) == (jj // (2 * b))
            lo, hi = (jj % (2 * b)) < b, (i % (2 * b)) >= b
            masks.append(same & (lo & hi if d == 0 else ~lo & ~hi))
    stm = np.stack(masks).astype(np.float32)
    return (jnp.asarray(bd, F32), jnp.asarray(bd, BF16), jnp.asarray(expand), jnp.asarray(tri), jnp.asarray(stm))


def kernel(x, c, ctx, c_ctx, ada_w, ada_b, norm1_g, ffn1_wg, ffn1_wu, ffn1_wd, norm2_g, w_in, swa_sink,
           dn_conv_w, dn_a_log, dn_dt_bias, dn_norm_g, mla_q_norm_g, mla_w_uq, mla_kv_norm_g, mla_w_ukv,
           na_rpb, w_out, norm3_g, ffn2_wg, ffn2_wu, ffn2_wd, final_norm_g):
    bsz, seq, d = x.shape
    depth = ada_w.shape[0]
    assert d == D_MODEL and ctx.shape[1] == CTX_LEN and seq % TOKEN_TILE == 0 and bsz <= CTX_ROW

    cvec = jnp.zeros((MOD_ROWS, d), F32).at[:bsz].set(c).at[CTX_ROW].set(c_ctx)
    mod = _modulation(cvec, ada_w, ada_b).reshape(depth, MOD_ROWS, 3, 3, d)

    cos_a, sin_a, cos_m, sin_m = _rope_tables(seq)
    band = _swa_band()
    ones_bd, bdm, expand, tri, stm = _dn_constants()
    in_cols = _inproj_columns()
    mq, mqs, mk, mv = _mla_columns()
    wo_rows = _wout_rows()
    row2 = lambda v: v.reshape(1, -1)

    h = jnp.concatenate([ctx, x], axis=1)
    for i in range(depth):
        last = i == depth - 1
        bf = lambda a: a.astype(BF16)
        h = _ffn(h, mod[i, :, 0], row2(norm1_g[i]), bf(ffn1_wg[i]), bf(ffn1_wu[i]), bf(ffn1_wd[i]))
        wp = bf(_gather_cols(w_in[i], in_cols))
        swa_q, swa_kv, dn_qkv, dn_z, dn_ab, mla_c, na_q, na_kv = _inproj(
            h, mod[i, :, 1], row2(norm2_g[i]), wp, cos_a, sin_a)

        ya = _swa(swa_q, swa_kv, swa_sink[i], band)
        yn = _na(na_q, na_kv, _na_bias_table(na_rpb[i]))
        mq_, mk_, mv_ = _mla_prep(
            mla_c, row2(mla_q_norm_g[i]), row2(mla_kv_norm_g[i]),
            bf(_gather_cols(mla_w_uq[i], mq)), bf(_gather_cols(mla_w_uq[i], mqs)),
            bf(_gather_cols(mla_w_ukv[i], mk)), bf(_gather_cols(mla_w_ukv[i], mv)), cos_m, sin_m)
        ym = _mla_attn(mq_, mk_, mv_)
        pad = lambda v: jnp.zeros((1, LANES), F32).at[0, :2 * N_HEADS].set(v.reshape(-1))
        u, w, qg, kd, intra, egl = _dn_local(dn_qkv, dn_ab, dn_conv_w[i], pad(dn_a_log[i]), pad(dn_dt_bias[i]),
                                             ones_bd, bdm, expand, tri, stm)
        o_f, o_b = _dn_scan(u, w, qg, kd, intra, egl, bdm)

        wo = bf(jnp.take(jnp.concatenate([w_out[i], jnp.zeros((1, d), F32)], axis=0), jnp.asarray(wo_rows), axis=0))
        mix = (ya, o_f, o_b, dn_z, ym, yn, mod[i, :, 1], row2(jnp.tile(dn_norm_g[i], N_HEADS)), ones_bd, wo)
        h = _ffn(h, mod[i, :, 2], row2(norm3_g[i]), bf(ffn2_wg[i]), bf(ffn2_wu[i]), bf(ffn2_wd[i]), mix=mix,
                 final_g=row2(final_norm_g) if last else None, latent_only=last)
    return h
```

```python
import functools

import numpy as np
import jax
import jax.numpy as jnp
from jax import lax
from jax.experimental import pallas as pl
from jax.experimental.pallas import tpu as pltpu

F32 = jnp.float32
BF16 = jnp.bfloat16
HIGHEST = lax.Precision.HIGHEST

D_MODEL = 1024
CTX_LEN = 256
GRID_W = 64
HEAD_DIM = 64
N_HEADS = 4
GROUP_W = N_HEADS * HEAD_DIM
D_FF = 2816
N_MOD = 9
ROPE_THETA = 10000.0
EPS = 1e-6
NEG_INF = -1e30
SWA_WINDOW = 128
SWA_BLOCK = 128
DN_CHUNK = 64
DN_LEVEL0 = 5
MLA_Q_RANK = 256
MLA_KV_RANK = 128
MLA_NOPE = 64
MLA_ROPE = 32
MLA_V = 64
NA_KR = 8
NA_KC = 16
IN_SIZES = (256, 128, 128, 768, 256, 8, 8, 256, 128, 32, 256, 256, 256)
IN_PROJ = sum(IN_SIZES)

LANES = 128
TOKEN_TILE = 256
VMEM_LIMIT = 56 * 1024 * 1024
MOD_ROWS = 16
CTX_ROW = 8

P_SWA_Q, P_SWA_QS, P_SWA_K, P_SWA_KS, P_SWA_V = 0, 256, 512, 640, 768
P_DN_QKV, P_DN_Z, P_DN_AB = 896, 1664, 1920
P_MLA = 2048
P_NA_Q, P_NA_KV = 2688, 2944
P_TOTAL = 3456


def _cparams(sem):
    return pltpu.CompilerParams(dimension_semantics=sem, vmem_limit_bytes=VMEM_LIMIT)


def _dot(a, b, precision=None):
    return jnp.dot(a, b, preferred_element_type=F32, precision=precision)


def _dot_nt(a, b):
    return lax.dot_general(a, b, (((1,), (1,)), ((), ())), preferred_element_type=F32)


def _dot_tn(a, b):
    return lax.dot_general(a, b, (((0,), (0,)), ((), ())), preferred_element_type=F32)


def _sigmoid(x):
    return 1.0 / (1.0 + jnp.exp(-x))


def _silu(x):
    return x * _sigmoid(x)


def _resident(shape):
    nd = len(shape)
    return pl.BlockSpec(shape, lambda *_: (0,) * nd, pipeline_mode=pl.Buffered(1))


def _mod_kernel(c_ref, w_ref, b_ref, o_ref):
    s = _silu(c_ref[...])
    o_ref[0] = _dot(s, w_ref[0], HIGHEST) + b_ref[0]


def _modulation(cvec, ada_w, ada_b):
    n_layers = ada_w.shape[0]
    d = D_MODEL
    return pl.pallas_call(
        _mod_kernel,
        grid=(n_layers, N_MOD),
        in_specs=[
            pl.BlockSpec((MOD_ROWS, d), lambda l, j: (0, 0)),
            pl.BlockSpec((1, d, d), lambda l, j: (l, 0, j)),
            pl.BlockSpec((1, 1, d), lambda l, j: (l, 0, j)),
        ],
        out_specs=pl.BlockSpec((1, MOD_ROWS, d), lambda l, j: (l, 0, j)),
        out_shape=jax.ShapeDtypeStruct((n_layers, MOD_ROWS, N_MOD * d), F32),
        compiler_params=_cparams(("arbitrary", "arbitrary")),
        name="modulation",
    )(cvec, ada_w, ada_b.reshape(n_layers, 1, N_MOD * d))


def _mod_index(b, i):
    return (jnp.where(i == 0, CTX_ROW, b), 0, 0)


def _prenorm(x, g, shift, scale):
    ms = jnp.mean(x * x, axis=-1, keepdims=True)
    return (x * lax.rsqrt(ms + EPS) * g) * (1.0 + scale) + shift


def _group_mean_sq(x, ones_bd):
    return _dot((x * x).astype(BF16), ones_bd) * (1.0 / HEAD_DIM)


def _ffn_kernel(*refs, mix, final):
    it = iter(refs)
    x_ref = next(it)
    if mix:
        ya_ref, of_ref, ob_ref, z_ref, ym_ref, yn_ref = (next(it) for _ in range(6))
        mmod_ref, dng_ref, ones_ref, wo_ref = (next(it) for _ in range(4))
    mod_ref, g_ref, wg_ref, wu_ref, wd_ref = (next(it) for _ in range(5))
    if final:
        fg_ref = next(it)
    o_ref = next(it)

    x = x_ref[0]
    if mix:
        o = of_ref[0] + ob_ref[0]
        yd = (o * lax.rsqrt(_group_mean_sq(o, ones_ref[...]) + EPS) * dng_ref[...]) * _silu(z_ref[0])
        ycat = jnp.concatenate(
            [ya_ref[0].astype(BF16), yd.astype(BF16), ym_ref[0].astype(BF16), yn_ref[0].astype(BF16)], axis=-1)
        x = x + mmod_ref[0][2:3] * _dot(ycat, wo_ref[...])
    mod = mod_ref[0]
    h = _prenorm(x, g_ref[...], mod[0:1], mod[1:2]).astype(BF16)
    a = (_silu(_dot(h, wg_ref[...])) * _dot(h, wu_ref[...])).astype(BF16)
    x = x + (0.5 * mod[2:3]) * _dot(a, wd_ref[...])
    if final:
        ms = jnp.mean(x * x, axis=-1, keepdims=True)
        x = x * lax.rsqrt(ms + EPS) * fg_ref[...]
    o_ref[0] = x


def _ffn(x, mod, g, wg, wu, wd, mix=None, final_g=None, latent_only=False):
    bsz, n, d = x.shape
    tm = TOKEN_TILE
    off = 1 if latent_only else 0
    tiles = n // tm - off
    tok = lambda w: pl.BlockSpec((1, tm, w), lambda b, i: (b, i + off, 0))
    modspec = pl.BlockSpec((1, 3, d), lambda b, i: _mod_index(b, i + off))
    args, specs = [x], [tok(d)]
    if mix is not None:
        ya, of, ob, z, ym, yn, mmod, dng, ones_bd, wo = mix
        args += [ya, of, ob, z, ym, yn, mmod, dng, ones_bd, wo]
        specs += [tok(ya.shape[-1]), tok(of.shape[-1]), tok(ob.shape[-1]), tok(z.shape[-1]),
                  tok(ym.shape[-1]), tok(yn.shape[-1]), modspec,
                  _resident(dng.shape), _resident(ones_bd.shape), _resident(wo.shape)]
    args += [mod, g, wg, wu, wd]
    specs += [modspec, _resident(g.shape), _resident(wg.shape), _resident(wu.shape), _resident(wd.shape)]
    if final_g is not None:
        args.append(final_g)
        specs.append(_resident(final_g.shape))
    return pl.pallas_call(
        functools.partial(_ffn_kernel, mix=mix is not None, final=final_g is not None),
        grid=(bsz, tiles),
        in_specs=specs,
        out_specs=pl.BlockSpec((1, tm, d), lambda b, i: (b, i, 0)),
        out_shape=jax.ShapeDtypeStruct((bsz, tiles * tm, d), F32),
        compiler_params=_cparams(("parallel", "arbitrary")),
        name="ffn_mix" if mix is not None else "ffn",
    )(*args)


def _inproj_kernel(x_ref, mod_ref, g_ref, w_ref, cos_ref, sin_ref,
                   swaq_ref, swakv_ref, dnqkv_ref, dnz_ref, dnab_ref, mla_ref, naq_ref, nakv_ref):
    mod = mod_ref[0]
    h = _prenorm(x_ref[0], g_ref[...], mod[0:1], mod[1:2]).astype(BF16)
    u = _dot(h, w_ref[...])
    cos, sin = cos_ref[...], sin_ref[...]
    cos2 = jnp.concatenate([cos, cos], axis=-1)
    sin2 = jnp.concatenate([sin, sin], axis=-1)
    scale = HEAD_DIM ** -0.5
    swaq_ref[0] = (u[:, P_SWA_Q:P_SWA_Q + 256] * cos2 + u[:, P_SWA_QS:P_SWA_QS + 256] * sin2) * scale
    k = u[:, P_SWA_K:P_SWA_K + 128] * cos + u[:, P_SWA_KS:P_SWA_KS + 128] * sin
    swakv_ref[0] = jnp.concatenate([k, u[:, P_SWA_V:P_SWA_V + 128]], axis=-1)
    dnqkv_ref[0] = u[:, P_DN_QKV:P_DN_QKV + 768]
    dnz_ref[0] = u[:, P_DN_Z:P_DN_Z + 256]
    dnab_ref[0] = u[:, P_DN_AB:P_DN_AB + 128]
    mla_ref[0] = u[:, P_MLA:P_MLA + 640]
    naq_ref[0] = u[:, P_NA_Q:P_NA_Q + 256] * scale
    nakv_ref[0] = u[:, P_NA_KV:P_NA_KV + 512]


def _inproj(x, mod, g, w, cos, sin):
    bsz, n, d = x.shape
    tm = TOKEN_TILE
    widths = (256, 256, 768, 256, 128, 640, 256, 512)
    tok = lambda w_: pl.BlockSpec((1, tm, w_), lambda b, i: (b, i, 0))
    tab = pl.BlockSpec((tm, LANES), lambda b, i: (i, 0))
    return pl.pallas_call(
        _inproj_kernel,
        grid=(bsz, n // tm),
        in_specs=[tok(d), pl.BlockSpec((1, 3, d), _mod_index), _resident(g.shape), _resident(w.shape), tab, tab],
        out_specs=[tok(w_) for w_ in widths],
        out_shape=[jax.ShapeDtypeStruct((bsz, n, w_), F32) for w_ in widths],
        compiler_params=_cparams(("parallel", "arbitrary")),
        name="inproj",
    )(x, mod, g, w, cos, sin)


def _swa_kernel(sink_ref, q_ref, kvp_ref, kvc_ref, kvn_ref, kvx_ref, band_ref, o_ref, *, first, last):
    j = pl.program_id(1)
    blk = SWA_BLOCK
    q = q_ref[0]
    lane = lax.broadcasted_iota(jnp.int32, (1, LANES), 1)
    hmask = [(lane < HEAD_DIM).astype(F32), (lane >= HEAD_DIM).astype(F32)]
    qs = jnp.concatenate([q[:, g * LANES:(g + 1) * LANES] * hmask[kh] for g in range(2) for kh in range(2)],
                         axis=0).astype(BF16)
    kvs = [kvp_ref[0], kvc_ref[0], kvn_ref[0], kvx_ref[0]]
    kcat = jnp.concatenate([t[:, :LANES] for t in kvs], axis=0).astype(BF16)
    vcat = jnp.concatenate([t[:, LANES:] for t in kvs], axis=0).astype(BF16)
    s = _dot_nt(qs, kcat) + band_ref[...]
    pen = [jnp.where(j > first, 0.0, NEG_INF), jnp.where(j >= first, 0.0, NEG_INF),
           jnp.where((j >= first) & (j < last), 0.0, NEG_INF)]
    zero = jnp.zeros((1, LANES), F32)
    s = s + jnp.concatenate([zero + pen[0], zero + pen[1], zero + pen[2], zero, zero], axis=-1)
    rblk = lax.broadcasted_iota(jnp.int32, (4 * blk, 1), 0) // blk
    sk = jnp.where(rblk == 0, sink_ref[0], jnp.where(rblk == 1, sink_ref[2],
                                                       jnp.where(rblk == 2, sink_ref[1], sink_ref[3])))
    m = jnp.maximum(jnp.max(s, axis=-1, keepdims=True), sk)
    p = jnp.exp(s - m)
    l = jnp.sum(p, axis=-1, keepdims=True) + jnp.exp(sk - m)
    o = _dot(p.astype(BF16), vcat) * (1.0 / l)
    outs = [o[(2 * g) * blk:(2 * g + 1) * blk] * hmask[0] + o[(2 * g + 1) * blk:(2 * g + 2) * blk] * hmask[1]
            for g in range(2)]
    o_ref[0] = jnp.concatenate(outs, axis=-1)


def _swa(q, kv, sink, band):
    bsz, n, _ = q.shape
    blk = SWA_BLOCK
    first = CTX_LEN // blk
    last = n // blk - 1
    clampi = lambda j: jnp.clip(j, first, last)
    tok = lambda f: pl.BlockSpec((1, blk, 256), f)
    return pl.pallas_call(
        functools.partial(_swa_kernel, first=first, last=last),
        grid=(bsz, n // blk),
        in_specs=[
            pl.BlockSpec(memory_space=pltpu.SMEM),
            tok(lambda b, j: (b, j, 0)),
            tok(lambda b, j: (b, clampi(j - 1), 0)),
            tok(lambda b, j: (b, clampi(j), 0)),
            tok(lambda b, j: (b, clampi(j + 1), 0)),
            pl.BlockSpec((1, CTX_LEN, 256), lambda b, j: (b, 0, 0)),
            _resident(band.shape),
        ],
        out_specs=tok(lambda b, j: (b, j, 0)),
        out_shape=jax.ShapeDtypeStruct((bsz, n, 256), F32),
        compiler_params=_cparams(("parallel", "arbitrary")),
        name="swa",
    )(sink, q, kv, kv, kv, kv, band)


def _na_kernel(q_ref, kv_ref, tab_ref, o_ref, *, rows):
    j = pl.program_id(1)
    w = GRID_W
    nctx = CTX_LEN // w
    r = jnp.maximum(j - nctx, 0)
    rs = jnp.clip(r - NA_KR // 2, 0, rows - NA_KR)
    tab_i = jnp.where(j < nctx, NA_KR, r - rs)
    q = q_ref[0]
    lane = lax.broadcasted_iota(jnp.int32, (1, GROUP_W), 1) // HEAD_DIM
    hmask = [(lane == h).astype(F32) for h in range(N_HEADS)]
    qs = jnp.concatenate([q * hmask[h] for h in range(N_HEADS)], axis=0).astype(BF16)
    start = pl.multiple_of(CTX_LEN + rs * w, w)
    kv_loc = kv_ref[0, pl.ds(start, NA_KR * w), :]
    kv_ctx = kv_ref[0, pl.ds(0, CTX_LEN), :]
    kcat = jnp.concatenate([kv_loc[:, :GROUP_W], kv_ctx[:, :GROUP_W]], axis=0).astype(BF16)
    vcat = jnp.concatenate([kv_loc[:, GROUP_W:], kv_ctx[:, GROUP_W:]], axis=0).astype(BF16)
    s = _dot_nt(qs, kcat)
    nloc = NA_KR * w
    s = jnp.concatenate([s[:, :nloc] + tab_ref[tab_i], s[:, nloc:]], axis=-1)
    m = jnp.max(s, axis=-1, keepdims=True)
    p = jnp.exp(s - m)
    l = jnp.sum(p, axis=-1, keepdims=True)
    o = _dot(p.astype(BF16), vcat) * (1.0 / l)
    acc = o[0:w] * hmask[0]
    for h in range(1, N_HEADS):
        acc = acc + o[h * w:(h + 1) * w] * hmask[h]
    o_ref[0] = acc


def _na(q, kv, tab):
    bsz, n, _ = q.shape
    w = GRID_W
    rows = (n - CTX_LEN) // w
    assert rows >= NA_KR
    return pl.pallas_call(
        functools.partial(_na_kernel, rows=rows),
        grid=(bsz, n // w),
        in_specs=[
            pl.BlockSpec((1, w, 256), lambda b, j: (b, j, 0)),
            pl.BlockSpec((1, n, 512), lambda b, j: (b, 0, 0)),
            _resident(tab.shape),
        ],
        out_specs=pl.BlockSpec((1, w, 256), lambda b, j: (b, j, 0)),
        out_shape=jax.ShapeDtypeStruct((bsz, n, 256), F32),
        compiler_params=_cparams(("parallel", "arbitrary")),
        name="na",
    )(q, kv, tab)


def _mla_prep_kernel(c_ref, gq_ref, gkv_ref, wq_ref, wqs_ref, wk_ref, wv_ref, cos_ref, sin_ref,
                     q_ref, k_ref, v_ref):
    c = c_ref[0]
    cq, ckv = c[:, 0:256], c[:, 256:384]
    kra, krb = c[:, 384:512], c[:, 512:640]
    nq = (cq * lax.rsqrt(jnp.mean(cq * cq, axis=-1, keepdims=True) + EPS) * gq_ref[...]).astype(BF16)
    nkv = (ckv * lax.rsqrt(jnp.mean(ckv * ckv, axis=-1, keepdims=True) + EPS) * gkv_ref[...]).astype(BF16)
    cos, sin = cos_ref[...], sin_ref[...]
    cos4 = jnp.concatenate([cos] * N_HEADS, axis=-1)
    sin4 = jnp.concatenate([sin] * N_HEADS, axis=-1)
    scale = (MLA_NOPE + MLA_ROPE) ** -0.5
    q_ref[0] = ((_dot(nq, wq_ref[...]) * cos4 + _dot(nq, wqs_ref[...]) * sin4) * scale).astype(BF16)
    kr = kra * cos + krb * sin
    k_ref[0] = (_dot(nkv, wk_ref[...]) + jnp.concatenate([kr] * N_HEADS, axis=-1)).astype(BF16)
    v_ref[0] = _dot(nkv, wv_ref[...]).astype(BF16)


def _mla_prep(c, gq, gkv, wq, wqs, wk, wv, cos, sin):
    bsz, n, _ = c.shape
    tm = TOKEN_TILE
    tok = lambda w_: pl.BlockSpec((1, tm, w_), lambda b, i: (b, i, 0))
    tab = pl.BlockSpec((tm, LANES), lambda b, i: (i, 0))
    return pl.pallas_call(
        _mla_prep_kernel,
        grid=(bsz, n // tm),
        in_specs=[tok(640)] + [_resident(a.shape) for a in (gq, gkv, wq, wqs, wk, wv)] + [tab, tab],
        out_specs=[tok(512)] * 3,
        out_shape=[jax.ShapeDtypeStruct((bsz, n, 512), BF16)] * 3,
        compiler_params=_cparams(("parallel", "arbitrary")),
        name="mla_prep",
    )(c, gq, gkv, wq, wqs, wk, wv, cos, sin)


def _mla_attn_kernel(q_ref, k_ref, v_ref, o_ref):
    i = pl.program_id(2)

    def attend(nkeys):
        q = q_ref[0]
        k = k_ref[0, pl.ds(0, nkeys), :]
        v = v_ref[0, pl.ds(0, nkeys), :]
        s = _dot_nt(q, k)
        m = jnp.max(s, axis=-1, keepdims=True)
        p = jnp.exp(s - m)
        l = jnp.sum(p, axis=-1, keepdims=True)
        o_ref[0] = _dot(p.astype(BF16), v) * (1.0 / l)

    @pl.when(i == 0)
    def _():
        attend(CTX_LEN)

    @pl.when(i > 0)
    def _():
        attend(k_ref.shape[1])


def _mla_attn(q, k, v):
    bsz, n, _ = q.shape
    tq = TOKEN_TILE
    return pl.pallas_call(
        _mla_attn_kernel,
        grid=(bsz, N_HEADS, n // tq),
        in_specs=[
            pl.BlockSpec((1, tq, LANES), lambda b, h, i: (b, i, h)),
            pl.BlockSpec((1, n, LANES), lambda b, h, i: (b, 0, h)),
            pl.BlockSpec((1, n, LANES), lambda b, h, i: (b, 0, h)),
        ],
        out_specs=pl.BlockSpec((1, tq, LANES), lambda b, h, i: (b, i, h)),
        out_shape=jax.ShapeDtypeStruct((bsz, n, N_HEADS * LANES), F32),
        compiler_params=_cparams(("parallel", "arbitrary", "arbitrary")),
        name="mla_attn",
    )(q, k, v)


def _bd(m, bd_mask):
    return jnp.concatenate([m] * N_HEADS, axis=0) * bd_mask


def _dn_local_kernel(x_ref, xp_ref, xn_ref, ab_ref, cw_ref, alog_ref, dtb_ref, bdm_ref, exp_ref, stm_ref,
                     u_ref, w_ref, qg_ref, kd_ref, in_ref, egl_ref):
    i = pl.program_id(1)
    nt = pl.num_programs(1)
    tm = TOKEN_TILE
    c = DN_CHUNK
    nch = tm // c
    x = x_ref[0]
    row = lax.broadcasted_iota(jnp.int32, (tm, 1), 0)
    has_prev = (i > 1).astype(F32)
    has_next = ((i > 0) & (i < nt - 1)).astype(F32)
    prev_row = xp_ref[0][7:8] * has_prev
    next_row = xn_ref[0][0:1] * has_next
    xm1 = jnp.where(row == 0, prev_row, pltpu.roll(x, 1, 0))
    xp1 = jnp.where(row == tm - 1, next_row, pltpu.roll(x, tm - 1, 0))
    cw = cw_ref[...]
    h = _silu(cw[0:1] * xm1 + cw[1:2] * x + cw[2:3] * xp1)
    q, k, v = h[:, 0:256], h[:, 256:512], h[:, 512:768]
    bdm = bdm_ref[...]
    q = q * lax.rsqrt(_dot((q * q).astype(BF16), bdm) + EPS) * (HEAD_DIM ** -0.5)
    k = k * lax.rsqrt(_dot((k * k).astype(BF16), bdm) + EPS)

    ab = ab_ref[0]
    z = ab + dtb_ref[...]
    softplus = jnp.maximum(z, 0.0) + jnp.log(1.0 + jnp.exp(-jnp.abs(z)))
    lane = lax.broadcasted_iota(jnp.int32, (1, LANES), 1)
    t = jnp.where(lane < 2 * N_HEADS, -jnp.exp(alog_ref[...]) * softplus, _sigmoid(ab))
    t = jnp.where(lane < 4 * N_HEADS, t, 0.0)
    rowc = row % c
    pre, suf = t, t
    step = 1
    while step < c:
        pre = pre + jnp.where(rowc >= step, pltpu.roll(pre, step, 0), 0.0)
        suf = suf + jnp.where(rowc < c - step, pltpu.roll(suf, tm - step, 0), 0.0)
        step *= 2
    t = jnp.where(lane < N_HEADS, pre, jnp.where(lane < 2 * N_HEADS, suf, t))
    hi = t.astype(BF16).astype(F32)
    rem = t - hi
    mid = rem.astype(BF16).astype(F32)
    comb = (hi + pltpu.roll(mid, 4 * N_HEADS, 1) + pltpu.roll(rem - mid, 8 * N_HEADS, 1)).astype(BF16)
    ex = _dot(comb, exp_ref[...])
    gcols = [ex[:, 0:256], ex[:, 256:512]]
    bexps = [ex[:, 512:768], ex[:, 768:1024]]

    eye = stm_ref[4]
    tile_b = eye.astype(BF16)
    chains = [(d, ci) for ci in range(nch) for d in range(2)]
    rows = lambda ci: slice(ci * c, (ci + 1) * c)
    kq = []
    for ci in range(nch):
        kc = k[rows(ci)]
        kt_bd = _dot_tn(kc.astype(BF16), tile_b).astype(BF16) * bdm
        lhs = jnp.concatenate([kc * bexps[0][rows(ci)], kc * bexps[1][rows(ci)], q[rows(ci)]], axis=0)
        kq.append(_dot(lhs.astype(BF16), kt_bd))
    lbs, xinvs = {}, {}
    for d, ci in chains:
        sl = rows(ci)
        gcol = gcols[d][sl]
        grow = jnp.sum(gcol * eye, axis=0, keepdims=True)
        incl, strict = stm_ref[2 * d], stm_ref[2 * d + 1]
        dec_incl = jnp.exp((gcol - grow) * incl) * incl
        lmat = kq[ci][d * c:(d + 1) * c] * (dec_incl * strict)
        egc = jnp.exp(gcol)
        glast = gcol[c - 1:c] if d == 0 else gcol[0:1]
        in_ref[d, 0, sl, :] = (kq[ci][2 * c:3 * c] * dec_incl).astype(BF16)
        qg_ref[d, 0, sl, :] = (q[sl] * egc).astype(BF16)
        kd_ref[d, 0, sl, :] = (k[sl] * jnp.exp(glast - gcol)).astype(BF16)
        egl_ref[d, 0, ci * 8:(ci + 1) * 8, :] = jnp.broadcast_to(jnp.exp(glast), (8, GROUP_W))
        lbs[d, ci] = lmat.astype(BF16)
        xinvs[d, ci] = eye - lmat * stm_ref[DN_LEVEL0 + 6 * d]
    for lvl in range(1, 6):
        ys = {}
        for d, ci in chains:
            off = _bd(lbs[d, ci] * stm_ref[DN_LEVEL0 + 6 * d + lvl].astype(BF16), bdm)
            ys[d, ci] = _dot(xinvs[d, ci].astype(BF16), off)
        for d, ci in chains:
            xinvs[d, ci] = xinvs[d, ci] - _dot(ys[d, ci].astype(BF16), _bd(xinvs[d, ci].astype(BF16), bdm))
    for d, ci in chains:
        sl = rows(ci)
        xb = xinvs[d, ci].astype(BF16)
        kb = k[sl] * bexps[d][sl]
        u_ref[d, 0, sl, :] = _dot(xb, _bd((v[sl] * bexps[d][sl]).astype(BF16), bdm))
        w_ref[d, 0, sl, :] = _dot(xb, _bd((kb * jnp.exp(gcols[d][sl])).astype(BF16), bdm)).astype(BF16)


def _dn_local(qkv, ab, cw, alog, dtb, bdm, expand, stm):
    bsz, n, _ = qkv.shape
    tm = TOKEN_TILE
    nt = n // tm
    r8 = tm // 8
    tok = lambda w_: pl.BlockSpec((1, tm, w_), lambda b, i: (b, i, 0))
    dtok = lambda: pl.BlockSpec((2, 1, tm, 256), lambda b, i: (0, b, i, 0))
    outs = [jax.ShapeDtypeStruct((2, bsz, n, 256), dt) for dt in (F32, BF16, BF16, BF16, BF16)]
    outs.append(jax.ShapeDtypeStruct((2, bsz, n // 8, 256), F32))
    return pl.pallas_call(
        _dn_local_kernel,
        grid=(bsz, nt),
        in_specs=[
            tok(768),
            pl.BlockSpec((1, 8, 768), lambda b, i: (b, jnp.maximum(i * r8 - 1, 0), 0)),
            pl.BlockSpec((1, 8, 768), lambda b, i: (b, jnp.minimum((i + 1) * r8, nt * r8 - 1), 0)),
            tok(128),
        ] + [_resident(a.shape) for a in (cw, alog, dtb, bdm, expand, stm)],
        out_specs=[dtok() for _ in range(5)] + [pl.BlockSpec((2, 1, tm // 8, 256), lambda b, i: (0, b, i, 0))],
        out_shape=outs,
        compiler_params=_cparams(("parallel", "arbitrary")),
        name="dn_local",
    )(qkv, qkv, qkv, ab, cw, alog, dtb, bdm, expand, stm)


def _dn_scan_kernel(uf_ref, wf_ref, qf_ref, kf_ref, if_ref, ef_ref,
                    ub_ref, wb_ref, qb_ref, kb_ref, ib_ref, eb_ref, bdm_ref,
                    of_ref, ob_ref, sf_ref, sb_ref):
    s_idx = pl.program_id(1)
    c = DN_CHUNK
    nch = TOKEN_TILE // c

    @pl.when(s_idx == 0)
    def _():
        sf_ref[...] = jnp.zeros_like(sf_ref)
        sb_ref[...] = jnp.zeros_like(sb_ref)

    bdm = bdm_ref[...]
    bdm_f = bdm.astype(F32)

    def chunk(refs, o_ref, s_ref, ci):
        u_ref, w_ref, q_ref, k_ref, i_ref, e_ref = refs
        sl = slice(ci * c, (ci + 1) * c)
        state = s_ref[...]
        ws = _dot(jnp.concatenate([w_ref[0, 0, sl, :], q_ref[0, 0, sl, :]], axis=0), state.astype(BF16))
        vnew = u_ref[0, 0, sl, :] - ws[:c]
        vb = vnew.astype(BF16)
        o_ref[0, sl, :] = ws[c:] + _dot(i_ref[0, 0, sl, :], _bd(vb, bdm))
        egl = e_ref[0, 0, ci * 8:ci * 8 + 1, :]
        s_ref[...] = state * egl + _dot_tn(k_ref[0, 0, sl, :], vb) * bdm_f

    fwd = (uf_ref, wf_ref, qf_ref, kf_ref, if_ref, ef_ref)
    bwd = (ub_ref, wb_ref, qb_ref, kb_ref, ib_ref, eb_ref)
    for ci in range(nch):
        chunk(fwd, of_ref, sf_ref, ci)
        chunk(bwd, ob_ref, sb_ref, nch - 1 - ci)


def _dn_scan(u, w, qg, kd, intra, egl, bdm):
    _, bsz, n, _ = u.shape
    tm = TOKEN_TILE
    nt = n // tm
    bidx = lambda s: jnp.where(s == 0, 0, nt - s)
    f4 = lambda w_: pl.BlockSpec((1, 1, w_, 256), lambda b, s: (0, b, s, 0))
    b4 = lambda w_: pl.BlockSpec((1, 1, w_, 256), lambda b, s: (1, b, bidx(s), 0))
    arrs = (u, w, qg, kd, intra)
    return pl.pallas_call(
        _dn_scan_kernel,
        grid=(bsz, nt),
        in_specs=[f4(tm)] * 5 + [f4(tm // 8)] + [b4(tm)] * 5 + [b4(tm // 8)] + [_resident(bdm.shape)],
        out_specs=[pl.BlockSpec((1, tm, 256), lambda b, s: (b, s, 0)),
                   pl.BlockSpec((1, tm, 256), lambda b, s: (b, bidx(s), 0))],
        out_shape=[jax.ShapeDtypeStruct((bsz, n, 256), F32)] * 2,
        scratch_shapes=[pltpu.VMEM((GROUP_W, GROUP_W), F32)] * 2,
        compiler_params=_cparams(("parallel", "arbitrary")),
        name="dn_scan",
    )(*arrs, egl, *arrs, egl, bdm)


def _half_swap(n):
    q = n // 4
    i = np.arange(n)
    return np.where((i % (2 * q)) < q, i + q, i - q)


def _inproj_columns():
    off = np.cumsum((0,) + IN_SIZES)
    z = IN_PROJ
    sw64, sw32 = _half_swap(64), _half_swap(32)
    gg, kk, dd = np.meshgrid(np.arange(2), np.arange(2), np.arange(64), indexing="ij")
    swa_q = (off[0] + kk * 128 + gg * 64 + dd).reshape(-1)
    swa_qs = (off[0] + kk * 128 + gg * 64 + sw64[dd]).reshape(-1)
    kk2, dd2 = np.meshgrid(np.arange(2), np.arange(64), indexing="ij")
    swa_k = (off[1] + kk2 * 64 + dd2).reshape(-1)
    swa_ks = (off[1] + kk2 * 64 + sw64[dd2]).reshape(-1)
    zeros = lambda n: np.full((n,), z)
    cols = np.concatenate([
        swa_q, swa_qs, swa_k, swa_ks, off[2] + np.arange(128),
        off[3] + np.arange(768), off[4] + np.arange(256),
        off[5] + np.arange(8), off[6] + np.arange(8), zeros(112),
        off[7] + np.arange(256), off[8] + np.arange(128),
        zeros(64), off[9] + np.arange(32), zeros(32),
        zeros(64), off[9] + sw32, zeros(32),
        off[10] + np.arange(256), off[11] + np.arange(256), off[12] + np.arange(256),
    ])
    assert cols.shape == (P_TOTAL,)
    return cols


def _gather_cols(w, cols):
    wz = jnp.concatenate([w, jnp.zeros(w.shape[:-1] + (1,), w.dtype)], axis=-1)
    return jnp.take(wz, jnp.asarray(cols), axis=-1)


def _mla_columns():
    dqk = MLA_NOPE + MLA_ROPE
    zq, zkv = N_HEADS * dqk, N_HEADS * (MLA_NOPE + MLA_V)
    sw32 = _half_swap(32)
    wq, wqs, wk, wv = [], [], [], []
    for h in range(N_HEADS):
        wq += [h * dqk + np.arange(dqk), np.full((32,), zq)]
        wqs += [np.full((64,), zq), h * dqk + MLA_NOPE + sw32, np.full((32,), zq)]
        wk += [h * 128 + np.arange(64), np.full((64,), zkv)]
        wv += [h * 128 + 64 + np.arange(64), np.full((64,), zkv)]
    return tuple(np.concatenate(c) for c in (wq, wqs, wk, wv))


def _wout_rows():
    z = 4 * GROUP_W
    gg, kk, dd = np.meshgrid(np.arange(2), np.arange(2), np.arange(64), indexing="ij")
    ya = (kk * 128 + gg * 64 + dd).reshape(-1)
    ym = np.concatenate([np.concatenate([2 * GROUP_W + h * 64 + np.arange(64), np.full((64,), z)])
                         for h in range(N_HEADS)])
    return np.concatenate([ya, GROUP_W + np.arange(256), ym, 3 * GROUP_W + np.arange(256)])


def _rope_tables(seq):
    t = jnp.arange(seq)
    row = (t // GRID_W).astype(F32)
    col = (t % GRID_W).astype(F32)

    def cs(n):
        inv = ROPE_THETA ** (-jnp.arange(0, n, 2, dtype=F32) / n)
        ar, ac = row[:, None] * inv[None, :], col[:, None] * inv[None, :]
        cos = jnp.concatenate([jnp.cos(ar)] * 2 + [jnp.cos(ac)] * 2, axis=-1)
        sin = jnp.concatenate([-jnp.sin(ar), jnp.sin(ar), -jnp.sin(ac), jnp.sin(ac)], axis=-1)
        return cos, sin

    cos64, sin64 = cs(HEAD_DIM // 2)
    cos32, sin32 = cs(MLA_ROPE // 2)
    ones = lambda w_: jnp.ones((CTX_LEN, w_), F32)
    zeros = lambda w_: jnp.zeros((CTX_LEN, w_), F32)
    cos_a = jnp.concatenate([ones(128), jnp.concatenate([cos64, cos64], axis=-1)], axis=0)
    sin_a = jnp.concatenate([zeros(128), jnp.concatenate([sin64, sin64], axis=-1)], axis=0)
    pad1, pad0 = jnp.ones((seq, 64), F32), jnp.zeros((seq, 32), F32)
    cos_m = jnp.concatenate([jnp.concatenate([ones(96), zeros(32)], axis=-1),
                             jnp.concatenate([pad1, cos32, pad0], axis=-1)], axis=0)
    sin_m = jnp.concatenate([zeros(128),
                             jnp.concatenate([jnp.zeros((seq, 64), F32), sin32, pad0], axis=-1)], axis=0)
    return cos_a, sin_a, cos_m, sin_m


def _swa_band():
    i = np.arange(4 * SWA_BLOCK)[:, None] % SWA_BLOCK
    j = np.arange(3 * SWA_BLOCK + CTX_LEN)[None, :]
    ok = (j >= 3 * SWA_BLOCK) | ((j >= i) & (j <= i + 2 * SWA_WINDOW))
    return jnp.asarray(np.where(ok, 0.0, NEG_INF), F32)


def _na_bias_table(rpb):
    w = GRID_W
    qc = np.arange(w)[:, None]
    kc = np.arange(w)[None, :]
    cs = np.clip(qc - NA_KC // 2, 0, w - NA_KC)
    ok = (kc >= cs) & (kc < cs + NA_KC)
    dc = np.clip(kc - qc + NA_KC - 1, 0, 2 * NA_KC - 2)
    onehot = (np.arange(2 * NA_KC - 1)[None, :, None] == dc[:, None, :]).astype(np.float32)
    full = jnp.einsum("hrd,qdk->hrqk", rpb, jnp.asarray(onehot), precision=HIGHEST)
    full = jnp.where(jnp.asarray(ok)[None, None], full, NEG_INF)
    tabs = [jnp.transpose(full[:, NA_KR - 1 - dl:2 * NA_KR - 1 - dl], (0, 2, 1, 3)).reshape(N_HEADS * w, NA_KR * w)
            for dl in range(NA_KR)]
    tabs.append(jnp.full((N_HEADS * w, NA_KR * w), NEG_INF, F32))
    return jnp.stack(tabs)


def _dn_constants():
    c = DN_CHUNK
    hh = np.arange(GROUP_W) // HEAD_DIM
    bd = (hh[:, None] == hh[None, :])
    expand = np.zeros((LANES, 4 * GROUP_W), np.float32)
    for piece in range(3):
        for d in range(2):
            expand[16 * piece + d * 4 + hh, d * GROUP_W + np.arange(GROUP_W)] = 1.0
            expand[16 * piece + 8 + d * 4 + hh, (2 + d) * GROUP_W + np.arange(GROUP_W)] = 1.0
    i = np.arange(c)[:, None]
    jj = (np.arange(GROUP_W) % c)[None, :]
    masks = [(jj <= i), (jj < i), (jj >= i), (jj > i), (jj == i)]
    for d in range(2):
        for lvl in range(6):
            b = 1 << lvl
            same = (i // (2 * b)) == (jj // (2 * b))
            lo, hi = (jj % (2 * b)) < b, (i % (2 * b)) >= b
            masks.append(same & (lo & hi if d == 0 else ~lo & ~hi))
    stm = np.stack(masks).astype(np.float32)
    return jnp.asarray(bd, BF16), jnp.asarray(expand, BF16), jnp.asarray(stm)


def kernel(x, c, ctx, c_ctx, ada_w, ada_b, norm1_g, ffn1_wg, ffn1_wu, ffn1_wd, norm2_g, w_in, swa_sink,
           dn_conv_w, dn_a_log, dn_dt_bias, dn_norm_g, mla_q_norm_g, mla_w_uq, mla_kv_norm_g, mla_w_ukv,
           na_rpb, w_out, norm3_g, ffn2_wg, ffn2_wu, ffn2_wd, final_norm_g):
    bsz, seq, d = x.shape
    depth = ada_w.shape[0]
    assert d == D_MODEL and ctx.shape[1] == CTX_LEN and seq % TOKEN_TILE == 0 and bsz <= CTX_ROW

    cvec = jnp.zeros((MOD_ROWS, d), F32).at[:bsz].set(c).at[CTX_ROW].set(c_ctx)
    mod = _modulation(cvec, ada_w, ada_b).reshape(depth, MOD_ROWS, 3, 3, d)

    cos_a, sin_a, cos_m, sin_m = _rope_tables(seq)
    band = _swa_band()
    bdm, expand, stm = _dn_constants()
    in_cols = _inproj_columns()
    mq, mqs, mk, mv = _mla_columns()
    wo_rows = _wout_rows()
    row2 = lambda v: v.reshape(1, -1)

    h = jnp.concatenate([ctx, x], axis=1)
    for i in range(depth):
        last = i == depth - 1
        bf = lambda a: a.astype(BF16)
        h = _ffn(h, mod[i, :, 0], row2(norm1_g[i]), bf(ffn1_wg[i]), bf(ffn1_wu[i]), bf(ffn1_wd[i]))
        wp = bf(_gather_cols(w_in[i], in_cols))
        swa_q, swa_kv, dn_qkv, dn_z, dn_ab, mla_c, na_q, na_kv = _inproj(
            h, mod[i, :, 1], row2(norm2_g[i]), wp, cos_a, sin_a)

        ya = _swa(swa_q, swa_kv, swa_sink[i], band)
        yn = _na(na_q, na_kv, _na_bias_table(na_rpb[i]))
        mq_, mk_, mv_ = _mla_prep(
            mla_c, row2(mla_q_norm_g[i]), row2(mla_kv_norm_g[i]),
            bf(_gather_cols(mla_w_uq[i], mq)), bf(_gather_cols(mla_w_uq[i], mqs)),
            bf(_gather_cols(mla_w_ukv[i], mk)), bf(_gather_cols(mla_w_ukv[i], mv)), cos_m, sin_m)
        ym = _mla_attn(mq_, mk_, mv_)
        pad = lambda v: jnp.zeros((1, LANES), F32).at[0, :2 * N_HEADS].set(v.reshape(-1))
        u, w, qg, kd, intra, egl = _dn_local(dn_qkv, dn_ab, dn_conv_w[i], pad(dn_a_log[i]), pad(dn_dt_bias[i]),
                                             bdm, expand, stm)
        o_f, o_b = _dn_scan(u, w, qg, kd, intra, egl, bdm)

        wo = bf(jnp.take(jnp.concatenate([w_out[i], jnp.zeros((1, d), F32)], axis=0), jnp.asarray(wo_rows), axis=0))
        mix = (ya, o_f, o_b, dn_z, ym, yn, mod[i, :, 1], row2(jnp.tile(dn_norm_g[i], N_HEADS)), bdm, wo)
        h = _ffn(h, mod[i, :, 2], row2(norm3_g[i]), bf(ffn2_wg[i]), bf(ffn2_wu[i]), bf(ffn2_wd[i]), mix=mix,
                 final_g=row2(final_norm_g) if last else None, latent_only=last)
    return h
```

```python
import functools

import numpy as np
import jax
import jax.numpy as jnp
from jax import lax
from jax.experimental import pallas as pl
from jax.experimental.pallas import tpu as pltpu

F32 = jnp.float32
BF16 = jnp.bfloat16
HIGHEST = lax.Precision.HIGHEST

D_MODEL = 1024
CTX_LEN = 256
GRID_W = 64
HEAD_DIM = 64
N_HEADS = 4
GROUP_W = N_HEADS * HEAD_DIM
D_FF = 2816
N_MOD = 9
ROPE_THETA = 10000.0
EPS = 1e-6
NEG_INF = -1e30
LOG2E = 1.4426950408889634
SWA_WINDOW = 128
SWA_BLOCK = 128
DN_CHUNK = 64
DN_LEVEL0 = 5
MLA_Q_RANK = 256
MLA_KV_RANK = 128
MLA_NOPE = 64
MLA_ROPE = 32
MLA_V = 64
NA_KR = 8
NA_KC = 16
IN_SIZES = (256, 128, 128, 768, 256, 8, 8, 256, 128, 32, 256, 256, 256)
IN_PROJ = sum(IN_SIZES)

LANES = 128
TOKEN_TILE = 256
VMEM_LIMIT = 56 * 1024 * 1024
MOD_ROWS = 16
CTX_ROW = 8

P_SWA_Q, P_SWA_QS, P_SWA_K, P_SWA_KS, P_SWA_V = 0, 256, 512, 640, 768
P_DN_QKV, P_DN_Z, P_DN_AB = 896, 1664, 1920
P_MLA = 2048
P_NA_Q, P_NA_KV = 2688, 2944
P_TOTAL = 3456


def _cparams(sem):
    return pltpu.CompilerParams(dimension_semantics=sem, vmem_limit_bytes=VMEM_LIMIT)


def _dot(a, b, precision=None):
    return jnp.dot(a, b, preferred_element_type=F32, precision=precision)


def _dot_nt(a, b):
    return lax.dot_general(a, b, (((1,), (1,)), ((), ())), preferred_element_type=F32)


def _dot_tn(a, b):
    return lax.dot_general(a, b, (((0,), (0,)), ((), ())), preferred_element_type=F32)


def _sigmoid(x):
    return 1.0 / (1.0 + jnp.exp(-x))


def _silu(x):
    return x * _sigmoid(x)


def _resident(shape):
    nd = len(shape)
    return pl.BlockSpec(shape, lambda *_: (0,) * nd, pipeline_mode=pl.Buffered(1))


def _mod_kernel(c_ref, w_ref, b_ref, o_ref):
    s = _silu(c_ref[...])
    o_ref[0] = _dot(s, w_ref[0], HIGHEST) + b_ref[0]


def _modulation(cvec, ada_w, ada_b):
    n_layers = ada_w.shape[0]
    d = D_MODEL
    return pl.pallas_call(
        _mod_kernel,
        grid=(n_layers, N_MOD),
        in_specs=[
            pl.BlockSpec((MOD_ROWS, d), lambda l, j: (0, 0)),
            pl.BlockSpec((1, d, d), lambda l, j: (l, 0, j)),
            pl.BlockSpec((1, 1, d), lambda l, j: (l, 0, j)),
        ],
        out_specs=pl.BlockSpec((1, MOD_ROWS, d), lambda l, j: (l, 0, j)),
        out_shape=jax.ShapeDtypeStruct((n_layers, MOD_ROWS, N_MOD * d), F32),
        compiler_params=_cparams(("arbitrary", "arbitrary")),
        name="modulation",
    )(cvec, ada_w, ada_b.reshape(n_layers, 1, N_MOD * d))


def _mod_index(b, i):
    return (jnp.where(i == 0, CTX_ROW, b), 0, 0)


def _prenorm(x, g, shift, scale):
    ms = jnp.mean(x * x, axis=-1, keepdims=True)
    return (x * lax.rsqrt(ms + EPS) * g) * (1.0 + scale) + shift


def _group_mean_sq(x, ones_bd):
    return _dot((x * x).astype(BF16), ones_bd) * (1.0 / HEAD_DIM)


def _ffn_kernel(*refs, mix, final, split):
    it = iter(refs)
    x_ref = next(it)
    if split:
        ctx_ref = next(it)
    if mix:
        ya_ref, of_ref, ob_ref, z_ref, ym_ref, yn_ref = (next(it) for _ in range(6))
        mmod_ref, dng_ref, ones_ref, wo_ref = (next(it) for _ in range(4))
    mod_ref, g_ref, wg_ref, wu_ref, wd_ref = (next(it) for _ in range(5))
    if final:
        fg_ref = next(it)
    o_ref = next(it)

    x = x_ref[0]
    if split:
        x = jnp.where(pl.program_id(1) == 0, ctx_ref[0], x)
    if mix:
        o = of_ref[0] + ob_ref[0]
        yd = (o * lax.rsqrt(_group_mean_sq(o, ones_ref[...]) + EPS) * dng_ref[...]) * _silu(z_ref[0])
        ycat = jnp.concatenate([ya_ref[0], yd.astype(BF16), ym_ref[0], yn_ref[0]], axis=-1)
        x = x + mmod_ref[0][2:3] * _dot(ycat, wo_ref[...])
    mod = mod_ref[0]
    h = _prenorm(x, g_ref[...], mod[0:1], mod[1:2]).astype(BF16)
    a = (_silu(_dot(h, wg_ref[...])) * _dot(h, wu_ref[...])).astype(BF16)
    x = x + (0.5 * mod[2:3]) * _dot(a, wd_ref[...])
    if final:
        ms = jnp.mean(x * x, axis=-1, keepdims=True)
        x = x * lax.rsqrt(ms + EPS) * fg_ref[...]
    o_ref[0] = x


def _ffn(x, mod, g, wg, wu, wd, mix=None, final_g=None, latent_only=False, ctx=None):
    bsz, n, d = x.shape
    tm = TOKEN_TILE
    off = 1 if latent_only else 0
    tiles = n // tm - off + (ctx is not None)
    tok = lambda w: pl.BlockSpec((1, tm, w), lambda b, i: (b, i + off, 0))
    modspec = pl.BlockSpec((1, 3, d), lambda b, i: _mod_index(b, i + off))
    if ctx is not None:
        assert mix is None and not latent_only and ctx.shape[1] == tm
        args = [x, ctx]
        specs = [pl.BlockSpec((1, tm, d), lambda b, i: (b, jnp.maximum(i - 1, 0), 0)),
                 pl.BlockSpec((1, tm, d), lambda b, i: (b, 0, 0))]
    else:
        args, specs = [x], [tok(d)]
    if mix is not None:
        ya, of, ob, z, ym, yn, mmod, dng, ones_bd, wo = mix
        args += [ya, of, ob, z, ym, yn, mmod, dng, ones_bd, wo]
        specs += [tok(ya.shape[-1]), tok(of.shape[-1]), tok(ob.shape[-1]), tok(z.shape[-1]),
                  tok(ym.shape[-1]), tok(yn.shape[-1]), modspec,
                  _resident(dng.shape), _resident(ones_bd.shape), _resident(wo.shape)]
    args += [mod, g, wg, wu, wd]
    specs += [modspec, _resident(g.shape), _resident(wg.shape), _resident(wu.shape), _resident(wd.shape)]
    if final_g is not None:
        args.append(final_g)
        specs.append(_resident(final_g.shape))
    return pl.pallas_call(
        functools.partial(_ffn_kernel, mix=mix is not None, final=final_g is not None, split=ctx is not None),
        grid=(bsz, tiles),
        in_specs=specs,
        out_specs=pl.BlockSpec((1, tm, d), lambda b, i: (b, i, 0)),
        out_shape=jax.ShapeDtypeStruct((bsz, tiles * tm, d), F32),
        compiler_params=_cparams(("parallel", "arbitrary")),
        name="ffn_mix" if mix is not None else "ffn",
    )(*args)


def _inproj_kernel(x_ref, mod_ref, g_ref, w_ref, cos_ref, sin_ref,
                   swaq_ref, swakv_ref, dnqkv_ref, dnz_ref, dnab_ref, mla_ref, naq_ref, nakv_ref):
    mod = mod_ref[0]
    h = _prenorm(x_ref[0], g_ref[...], mod[0:1], mod[1:2]).astype(BF16)
    u = _dot(h, w_ref[...])
    cos, sin = cos_ref[...], sin_ref[...]
    cos2 = jnp.concatenate([cos, cos], axis=-1)
    sin2 = jnp.concatenate([sin, sin], axis=-1)
    scale = HEAD_DIM ** -0.5
    swaq_ref[0] = ((u[:, P_SWA_Q:P_SWA_Q + 256] * cos2 + u[:, P_SWA_QS:P_SWA_QS + 256] * sin2) * scale).astype(BF16)
    k = u[:, P_SWA_K:P_SWA_K + 128] * cos + u[:, P_SWA_KS:P_SWA_KS + 128] * sin
    swakv_ref[0] = jnp.concatenate([k, u[:, P_SWA_V:P_SWA_V + 128]], axis=-1).astype(BF16)
    dnqkv_ref[0] = u[:, P_DN_QKV:P_DN_QKV + 768]
    dnz_ref[0] = u[:, P_DN_Z:P_DN_Z + 256]
    dnab_ref[0] = u[:, P_DN_AB:P_DN_AB + 128]
    mla_ref[0] = u[:, P_MLA:P_MLA + 640]
    naq_ref[0] = (u[:, P_NA_Q:P_NA_Q + 256] * scale).astype(BF16)
    nakv_ref[0] = u[:, P_NA_KV:P_NA_KV + 512].astype(BF16)


def _inproj(x, mod, g, w, cos, sin):
    bsz, n, d = x.shape
    tm = TOKEN_TILE
    widths = (256, 256, 768, 256, 128, 640, 256, 512)
    dtypes = (BF16, BF16, F32, F32, F32, F32, BF16, BF16)
    tok = lambda w_: pl.BlockSpec((1, tm, w_), lambda b, i: (b, i, 0))
    tab = pl.BlockSpec((tm, LANES), lambda b, i: (i, 0))
    return pl.pallas_call(
        _inproj_kernel,
        grid=(bsz, n // tm),
        in_specs=[tok(d), pl.BlockSpec((1, 3, d), _mod_index), _resident(g.shape), _resident(w.shape), tab, tab],
        out_specs=[tok(w_) for w_ in widths],
        out_shape=[jax.ShapeDtypeStruct((bsz, n, w_), dt) for w_, dt in zip(widths, dtypes)],
        compiler_params=_cparams(("parallel", "arbitrary")),
        name="inproj",
    )(x, mod, g, w, cos, sin)


def _swa_kernel(sink_ref, q_ref, kvp_ref, kvc_ref, kvn_ref, kvx_ref, band_ref, o_ref):
    i = pl.program_id(1)
    nt = pl.num_programs(1)
    blk = SWA_BLOCK
    lane = lax.broadcasted_iota(jnp.int32, (1, LANES), 1)
    hsel = [lane < HEAD_DIM, lane >= HEAD_DIM]
    hmask = [m.astype(F32) for m in hsel]
    q = q_ref[0]
    cur = kvc_ref[0]
    kvx = kvx_ref[0]
    latent = i > 0
    blocks = [(kvp_ref[0], cur[:blk], cur[blk:]), (cur[:blk], cur[blk:], kvn_ref[0])]
    valid = [(latent & (i > 1), latent, latent), (latent, latent, latent & (i < nt - 1))]
    rblk = lax.broadcasted_iota(jnp.int32, (4 * blk, 1), 0) // blk
    sk = jnp.where(rblk == 0, sink_ref[0], jnp.where(rblk == 1, sink_ref[2],
                                                       jnp.where(rblk == 2, sink_ref[1], sink_ref[3])))
    zero = jnp.zeros((1, LANES), F32)
    s, vcat = [], []
    for u in range(2):
        qu = q[u * blk:(u + 1) * blk]
        qs = jnp.concatenate([jnp.where(hsel[kh], qu[:, g * LANES:(g + 1) * LANES], 0)
                              for g in range(2) for kh in range(2)], axis=0)
        kvs = list(blocks[u]) + [kvx]
        kcat = jnp.concatenate([t[:, :LANES] for t in kvs], axis=0)
        vcat.append(jnp.concatenate([t[:, LANES:] for t in kvs], axis=0))
        pen = jnp.concatenate([zero + jnp.where(ok, 0.0, NEG_INF) for ok in valid[u]] + [zero, zero], axis=-1)
        s.append(_dot_nt(qs, kcat) + band_ref[...] + pen)
    p, l = [], []
    for u in range(2):
        m = jnp.maximum(jnp.max(s[u], axis=-1, keepdims=True), sk)
        pu = jnp.exp(s[u] - m)
        l.append(jnp.sum(pu, axis=-1, keepdims=True) + jnp.exp(sk - m))
        p.append(pu.astype(BF16))
    for u in range(2):
        o = _dot(p[u], vcat[u]) * (1.0 / l[u])
        outs = [o[(2 * g) * blk:(2 * g + 1) * blk] * hmask[0] + o[(2 * g + 1) * blk:(2 * g + 2) * blk] * hmask[1]
                for g in range(2)]
        o_ref[0, u * blk:(u + 1) * blk, :] = jnp.concatenate(outs, axis=-1).astype(o_ref.dtype)


def _swa(q, kv, sink, band):
    bsz, n, _ = q.shape
    blk = SWA_BLOCK
    tm = 2 * blk
    first = CTX_LEN // blk
    last = n // blk - 1
    return pl.pallas_call(
        _swa_kernel,
        grid=(bsz, n // tm),
        in_specs=[
            pl.BlockSpec(memory_space=pltpu.SMEM),
            pl.BlockSpec((1, tm, 256), lambda b, i: (b, i, 0)),
            pl.BlockSpec((1, blk, 256), lambda b, i: (b, jnp.clip(2 * i - 1, first, last), 0)),
            pl.BlockSpec((1, tm, 256), lambda b, i: (b, jnp.maximum(i, 1), 0)),
            pl.BlockSpec((1, blk, 256), lambda b, i: (b, jnp.clip(2 * i + 2, first, last), 0)),
            pl.BlockSpec((1, CTX_LEN, 256), lambda b, i: (b, 0, 0)),
            _resident(band.shape),
        ],
        out_specs=pl.BlockSpec((1, tm, 256), lambda b, i: (b, i, 0)),
        out_shape=jax.ShapeDtypeStruct((bsz, n, 256), BF16),
        compiler_params=_cparams(("parallel", "arbitrary")),
        name="swa",
    )(sink, q, kv, kv, kv, kv, band)


def _na_kernel(q_ref, kv_ref, tab_ref, o_ref, *, rows):
    i = pl.program_id(1)
    w = GRID_W
    nctx = CTX_LEN // w
    nloc = NA_KR * w
    lane = lax.broadcasted_iota(jnp.int32, (1, GROUP_W), 1) // HEAD_DIM
    hsel = [lane == h for h in range(N_HEADS)]
    hmask = [m.astype(F32) for m in hsel]
    kv_ctx = kv_ref[0, pl.ds(0, CTX_LEN), :]
    s, vcat = [], []
    for u in range(2):
        j = 2 * i + u
        r = jnp.maximum(j - nctx, 0)
        rs = jnp.clip(r - NA_KR // 2, 0, rows - NA_KR)
        tab_i = jnp.where(j < nctx, NA_KR, r - rs)
        qu = q_ref[0, u * w:(u + 1) * w, :]
        qs = jnp.concatenate([jnp.where(hsel[h], qu, 0) for h in range(N_HEADS)], axis=0)
        start = pl.multiple_of(CTX_LEN + rs * w, w)
        kv_loc = kv_ref[0, pl.ds(start, nloc), :]
        kcat = jnp.concatenate([kv_loc[:, :GROUP_W], kv_ctx[:, :GROUP_W]], axis=0)
        vcat.append(jnp.concatenate([kv_loc[:, GROUP_W:], kv_ctx[:, GROUP_W:]], axis=0))
        su = _dot_nt(qs, kcat)
        s.append(jnp.concatenate([su[:, :nloc] + tab_ref[tab_i], su[:, nloc:]], axis=-1))
    p, l = [], []
    for u in range(2):
        pu = jnp.exp(s[u] - jnp.max(s[u], axis=-1, keepdims=True))
        l.append(jnp.sum(pu, axis=-1, keepdims=True))
        p.append(pu.astype(BF16))
    for u in range(2):
        o = _dot(p[u], vcat[u]) * (1.0 / l[u])
        acc = o[0:w] * hmask[0]
        for h in range(1, N_HEADS):
            acc = acc + o[h * w:(h + 1) * w] * hmask[h]
        o_ref[0, u * w:(u + 1) * w, :] = acc.astype(o_ref.dtype)


def _na(q, kv, tab):
    bsz, n, _ = q.shape
    w = GRID_W
    rows = (n - CTX_LEN) // w
    assert rows >= NA_KR
    return pl.pallas_call(
        functools.partial(_na_kernel, rows=rows),
        grid=(bsz, n // (2 * w)),
        in_specs=[
            pl.BlockSpec((1, 2 * w, 256), lambda b, i: (b, i, 0)),
            pl.BlockSpec((1, n, 512), lambda b, i: (b, 0, 0)),
            _resident(tab.shape),
        ],
        out_specs=pl.BlockSpec((1, 2 * w, 256), lambda b, i: (b, i, 0)),
        out_shape=jax.ShapeDtypeStruct((bsz, n, 256), BF16),
        compiler_params=_cparams(("parallel", "arbitrary")),
        name="na",
    )(q, kv, tab)


def _mla_prep_kernel(c_ref, gq_ref, gkv_ref, wq_ref, wqs_ref, wk_ref, wv_ref, vone_ref, cos_ref, sin_ref,
                     q_ref, k_ref, v_ref):
    c = c_ref[0]
    cq, ckv = c[:, 0:256], c[:, 256:384]
    kra, krb = c[:, 384:512], c[:, 512:640]
    nq = (cq * lax.rsqrt(jnp.mean(cq * cq, axis=-1, keepdims=True) + EPS) * gq_ref[...]).astype(BF16)
    nkv = (ckv * lax.rsqrt(jnp.mean(ckv * ckv, axis=-1, keepdims=True) + EPS) * gkv_ref[...]).astype(BF16)
    cos, sin = cos_ref[...], sin_ref[...]
    cos4 = jnp.concatenate([cos] * N_HEADS, axis=-1)
    sin4 = jnp.concatenate([sin] * N_HEADS, axis=-1)
    scale = (MLA_NOPE + MLA_ROPE) ** -0.5 * LOG2E
    q_ref[0] = ((_dot(nq, wq_ref[...]) * cos4 + _dot(nq, wqs_ref[...]) * sin4) * scale).astype(BF16)
    kr = kra * cos + krb * sin
    k_ref[0] = (_dot(nkv, wk_ref[...]) + jnp.concatenate([kr] * N_HEADS, axis=-1)).astype(BF16)
    v_ref[0] = (_dot(nkv, wv_ref[...]) + vone_ref[...]).astype(BF16)


def _mla_prep(c, gq, gkv, wq, wqs, wk, wv, cos, sin):
    bsz, n, _ = c.shape
    tm = TOKEN_TILE
    vone = jnp.asarray((np.arange(N_HEADS * LANES) % LANES == MLA_V).astype(np.float32)).reshape(1, -1)
    tok = lambda w_: pl.BlockSpec((1, tm, w_), lambda b, i: (b, i, 0))
    tab = pl.BlockSpec((tm, LANES), lambda b, i: (i, 0))
    return pl.pallas_call(
        _mla_prep_kernel,
        grid=(bsz, n // tm),
        in_specs=[tok(640)] + [_resident(a.shape) for a in (gq, gkv, wq, wqs, wk, wv, vone)] + [tab, tab],
        out_specs=[tok(512)] * 3,
        out_shape=[jax.ShapeDtypeStruct((bsz, n, 512), BF16)] * 3,
        compiler_params=_cparams(("parallel", "arbitrary")),
        name="mla_prep",
    )(c, gq, gkv, wq, wqs, wk, wv, vone, cos, sin)


def _mla_attn_kernel(q_ref, k_ref, v_ref, o_ref):
    i = pl.program_id(1)

    def attend(nkeys):
        head = lambda ref, h, rows: ref[0, pl.ds(0, rows), h * LANES:(h + 1) * LANES]
        s, p = {}, {}
        for t in range(N_HEADS + 2):
            if t < N_HEADS:
                s[t] = _dot_nt(head(q_ref, t, TOKEN_TILE), head(k_ref, t, nkeys))
            if 0 <= t - 1 < N_HEADS:
                sh = s.pop(t - 1)
                p[t - 1] = jnp.exp2(sh - jnp.max(sh, axis=-1, keepdims=True)).astype(BF16)
            if 0 <= t - 2 < N_HEADS:
                h = t - 2
                o = _dot(p.pop(h), head(v_ref, h, nkeys))
                o_ref[0, :, h * LANES:(h + 1) * LANES] = (o * (1.0 / o[:, MLA_V:MLA_V + 1])).astype(o_ref.dtype)

    @pl.when(i == 0)
    def _():
        attend(CTX_LEN)

    @pl.when(i > 0)
    def _():
        attend(k_ref.shape[1])


def _mla_attn(q, k, v):
    bsz, n, w = q.shape
    tq = TOKEN_TILE
    return pl.pallas_call(
        _mla_attn_kernel,
        grid=(bsz, n // tq),
        in_specs=[
            pl.BlockSpec((1, tq, w), lambda b, i: (b, i, 0)),
            pl.BlockSpec((1, n, w), lambda b, i: (b, 0, 0)),
            pl.BlockSpec((1, n, w), lambda b, i: (b, 0, 0)),
        ],
        out_specs=pl.BlockSpec((1, tq, w), lambda b, i: (b, i, 0)),
        out_shape=jax.ShapeDtypeStruct((bsz, n, w), BF16),
        compiler_params=_cparams(("parallel", "arbitrary")),
        name="mla_attn",
    )(q, k, v)


def _bd(m, bd_mask):
    return jnp.concatenate([m] * N_HEADS, axis=0) * bd_mask


def _dn_local_kernel(x_ref, xp_ref, xn_ref, ab_ref, cw_ref, alog_ref, dtb_ref, bdm_ref, exp_ref, stm_ref,
                     u_ref, w_ref, qg_ref, kd_ref, in_ref, egl_ref):
    i = pl.program_id(1)
    nt = pl.num_programs(1)
    tm = TOKEN_TILE
    c = DN_CHUNK
    nch = tm // c
    x = x_ref[0]
    row = lax.broadcasted_iota(jnp.int32, (tm, 1), 0)
    has_prev = (i > 1).astype(F32)
    has_next = ((i > 0) & (i < nt - 1)).astype(F32)
    prev_row = xp_ref[0][7:8] * has_prev
    next_row = xn_ref[0][0:1] * has_next
    xm1 = jnp.where(row == 0, prev_row, pltpu.roll(x, 1, 0))
    xp1 = jnp.where(row == tm - 1, next_row, pltpu.roll(x, tm - 1, 0))
    cw = cw_ref[...]
    h = _silu(cw[0:1] * xm1 + cw[1:2] * x + cw[2:3] * xp1)
    q, k, v = h[:, 0:256], h[:, 256:512], h[:, 512:768]
    bdm = bdm_ref[...]
    q = q * lax.rsqrt(_dot((q * q).astype(BF16), bdm) + EPS) * (HEAD_DIM ** -0.5)
    k = k * lax.rsqrt(_dot((k * k).astype(BF16), bdm) + EPS)

    ab = ab_ref[0]
    z = ab + dtb_ref[...]
    softplus = jnp.maximum(z, 0.0) + jnp.log(1.0 + jnp.exp(-jnp.abs(z)))
    lane = lax.broadcasted_iota(jnp.int32, (1, LANES), 1)
    t = jnp.where(lane < 2 * N_HEADS, -jnp.exp(alog_ref[...]) * softplus, _sigmoid(ab))
    t = jnp.where(lane < 4 * N_HEADS, t, 0.0)
    rowc = row % c
    pre, suf = t, t
    step = 1
    while step < c:
        pre = pre + jnp.where(rowc >= step, pltpu.roll(pre, step, 0), 0.0)
        suf = suf + jnp.where(rowc < c - step, pltpu.roll(suf, tm - step, 0), 0.0)
        step *= 2
    t = jnp.where(lane < N_HEADS, pre, jnp.where(lane < 2 * N_HEADS, suf, t))
    hi = t.astype(BF16).astype(F32)
    rem = t - hi
    mid = rem.astype(BF16).astype(F32)
    comb = (hi + pltpu.roll(mid, 4 * N_HEADS, 1) + pltpu.roll(rem - mid, 8 * N_HEADS, 1)).astype(BF16)
    ex = _dot(comb, exp_ref[...])
    gcols = [ex[:, 0:256], ex[:, 256:512]]
    bexps = [ex[:, 512:768], ex[:, 768:1024]]

    eye = stm_ref[4]
    tile_b = eye.astype(BF16)
    chains = [(d, ci) for ci in range(nch) for d in range(2)]
    rows = lambda ci: slice(ci * c, (ci + 1) * c)
    kq = []
    for ci in range(nch):
        kc = k[rows(ci)]
        kt_bd = _dot_tn(kc.astype(BF16), tile_b).astype(BF16) * bdm
        lhs = jnp.concatenate([kc * bexps[0][rows(ci)], kc * bexps[1][rows(ci)], q[rows(ci)]], axis=0)
        kq.append(_dot(lhs.astype(BF16), kt_bd))
    lbs, xinvs = {}, {}
    for d, ci in chains:
        sl = rows(ci)
        gcol = gcols[d][sl]
        grow = jnp.sum(gcol * eye, axis=0, keepdims=True)
        incl, strict = stm_ref[2 * d], stm_ref[2 * d + 1]
        dec_incl = jnp.exp((gcol - grow) * incl) * incl
        lmat = kq[ci][d * c:(d + 1) * c] * (dec_incl * strict)
        egc = jnp.exp(gcol)
        glast = gcol[c - 1:c] if d == 0 else gcol[0:1]
        in_ref[d, 0, sl, :] = (kq[ci][2 * c:3 * c] * dec_incl).astype(BF16)
        qg_ref[d, 0, sl, :] = (q[sl] * egc).astype(BF16)
        kd_ref[d, 0, sl, :] = (k[sl] * jnp.exp(glast - gcol)).astype(BF16)
        egl_ref[d, 0, ci * 8:(ci + 1) * 8, :] = jnp.broadcast_to(jnp.exp(glast), (8, GROUP_W))
        lbs[d, ci] = lmat.astype(BF16)
        xinvs[d, ci] = eye - lmat * stm_ref[DN_LEVEL0 + 6 * d]
    for lvl in range(1, 6):
        ys = {}
        for d, ci in chains:
            off = _bd(lbs[d, ci] * stm_ref[DN_LEVEL0 + 6 * d + lvl].astype(BF16), bdm)
            ys[d, ci] = _dot(xinvs[d, ci].astype(BF16), off)
        for d, ci in chains:
            xinvs[d, ci] = xinvs[d, ci] - _dot(ys[d, ci].astype(BF16), _bd(xinvs[d, ci].astype(BF16), bdm))
    for d, ci in chains:
        sl = rows(ci)
        xb = xinvs[d, ci].astype(BF16)
        kb = k[sl] * bexps[d][sl]
        u_ref[d, 0, sl, :] = _dot(xb, _bd((v[sl] * bexps[d][sl]).astype(BF16), bdm))
        w_ref[d, 0, sl, :] = _dot(xb, _bd((kb * jnp.exp(gcols[d][sl])).astype(BF16), bdm)).astype(BF16)


def _dn_local(qkv, ab, cw, alog, dtb, bdm, expand, stm):
    bsz, n, _ = qkv.shape
    tm = TOKEN_TILE
    nt = n // tm
    r8 = tm // 8
    tok = lambda w_: pl.BlockSpec((1, tm, w_), lambda b, i: (b, i, 0))
    dtok = lambda: pl.BlockSpec((2, 1, tm, 256), lambda b, i: (0, b, i, 0))
    outs = [jax.ShapeDtypeStruct((2, bsz, n, 256), dt) for dt in (F32, BF16, BF16, BF16, BF16)]
    outs.append(jax.ShapeDtypeStruct((2, bsz, n // 8, 256), F32))
    return pl.pallas_call(
        _dn_local_kernel,
        grid=(bsz, nt),
        in_specs=[
            tok(768),
            pl.BlockSpec((1, 8, 768), lambda b, i: (b, jnp.maximum(i * r8 - 1, 0), 0)),
            pl.BlockSpec((1, 8, 768), lambda b, i: (b, jnp.minimum((i + 1) * r8, nt * r8 - 1), 0)),
            tok(128),
        ] + [_resident(a.shape) for a in (cw, alog, dtb, bdm, expand, stm)],
        out_specs=[dtok() for _ in range(5)] + [pl.BlockSpec((2, 1, tm // 8, 256), lambda b, i: (0, b, i, 0))],
        out_shape=outs,
        compiler_params=_cparams(("parallel", "arbitrary")),
        name="dn_local",
    )(qkv, qkv, qkv, ab, cw, alog, dtb, bdm, expand, stm)


def _dn_scan_kernel(uf_ref, wf_ref, qf_ref, kf_ref, if_ref, ef_ref,
                    ub_ref, wb_ref, qb_ref, kb_ref, ib_ref, eb_ref, bdm_ref,
                    of_ref, ob_ref, s_ref):
    c = DN_CHUNK
    nch = TOKEN_TILE // c
    nb = of_ref.shape[0]

    @pl.when(pl.program_id(1) == 0)
    def _():
        s_ref[...] = jnp.zeros_like(s_ref)

    bdm = bdm_ref[...]
    bdm_f = bdm.astype(F32)
    ins = ((uf_ref, wf_ref, qf_ref, kf_ref, if_ref, ef_ref), (ub_ref, wb_ref, qb_ref, kb_ref, ib_ref, eb_ref))
    outs = (of_ref, ob_ref)
    for step in range(nch):
        chains = [(d, bi, step if d == 0 else nch - 1 - step) for bi in range(nb) for d in range(2)]
        rows = lambda ci: slice(ci * c, (ci + 1) * c)
        ws, vb = {}, {}
        for d, bi, ci in chains:
            _, w_ref, q_ref, _, _, _ = ins[d]
            lhs = jnp.concatenate([w_ref[0, bi, rows(ci), :], q_ref[0, bi, rows(ci), :]], axis=0)
            ws[d, bi] = _dot(lhs, s_ref[d, bi].astype(BF16))
        for d, bi, ci in chains:
            u_ref, _, _, _, i_ref, _ = ins[d]
            vb[d, bi] = (u_ref[0, bi, rows(ci), :] - ws[d, bi][:c]).astype(BF16)
            outs[d][bi, rows(ci), :] = ws[d, bi][c:] + _dot(i_ref[0, bi, rows(ci), :], _bd(vb[d, bi], bdm))
        for d, bi, ci in chains:
            _, _, _, k_ref, _, e_ref = ins[d]
            egl = e_ref[0, bi, ci * 8:ci * 8 + 1, :]
            s_ref[d, bi] = s_ref[d, bi] * egl + _dot_tn(k_ref[0, bi, rows(ci), :], vb[d, bi]) * bdm_f


def _dn_scan(u, w, qg, kd, intra, egl, bdm):
    _, bsz, n, _ = u.shape
    tm = TOKEN_TILE
    nt = n // tm
    nb = 4 if bsz % 4 == 0 else (2 if bsz % 2 == 0 else 1)
    bidx = lambda s: jnp.where(s == 0, 0, nt - s)
    f4 = lambda w_: pl.BlockSpec((1, nb, w_, 256), lambda b, s: (0, b, s, 0))
    b4 = lambda w_: pl.BlockSpec((1, nb, w_, 256), lambda b, s: (1, b, bidx(s), 0))
    arrs = (u, w, qg, kd, intra)
    return pl.pallas_call(
        _dn_scan_kernel,
        grid=(bsz // nb, nt),
        in_specs=[f4(tm)] * 5 + [f4(tm // 8)] + [b4(tm)] * 5 + [b4(tm // 8)] + [_resident(bdm.shape)],
        out_specs=[pl.BlockSpec((nb, tm, 256), lambda b, s: (b, s, 0)),
                   pl.BlockSpec((nb, tm, 256), lambda b, s: (b, bidx(s), 0))],
        out_shape=[jax.ShapeDtypeStruct((bsz, n, 256), F32)] * 2,
        scratch_shapes=[pltpu.VMEM((2, nb, GROUP_W, GROUP_W), F32)],
        compiler_params=_cparams(("parallel", "arbitrary")),
        name="dn_scan",
    )(*arrs, egl, *arrs, egl, bdm)


def _half_swap(n):
    q = n // 4
    i = np.arange(n)
    return np.where((i % (2 * q)) < q, i + q, i - q)


def _inproj_columns():
    off = np.cumsum((0,) + IN_SIZES)
    z = IN_PROJ
    sw64, sw32 = _half_swap(64), _half_swap(32)
    gg, kk, dd = np.meshgrid(np.arange(2), np.arange(2), np.arange(64), indexing="ij")
    swa_q = (off[0] + kk * 128 + gg * 64 + dd).reshape(-1)
    swa_qs = (off[0] + kk * 128 + gg * 64 + sw64[dd]).reshape(-1)
    kk2, dd2 = np.meshgrid(np.arange(2), np.arange(64), indexing="ij")
    swa_k = (off[1] + kk2 * 64 + dd2).reshape(-1)
    swa_ks = (off[1] + kk2 * 64 + sw64[dd2]).reshape(-1)
    zeros = lambda n: np.full((n,), z)
    cols = np.concatenate([
        swa_q, swa_qs, swa_k, swa_ks, off[2] + np.arange(128),
        off[3] + np.arange(768), off[4] + np.arange(256),
        off[5] + np.arange(8), off[6] + np.arange(8), zeros(112),
        off[7] + np.arange(256), off[8] + np.arange(128),
        zeros(64), off[9] + np.arange(32), zeros(32),
        zeros(64), off[9] + sw32, zeros(32),
        off[10] + np.arange(256), off[11] + np.arange(256), off[12] + np.arange(256),
    ])
    assert cols.shape == (P_TOTAL,)
    return cols


def _gather_cols(w, cols):
    wz = jnp.concatenate([w, jnp.zeros(w.shape[:-1] + (1,), w.dtype)], axis=-1)
    return jnp.take(wz, jnp.asarray(cols), axis=-1)


def _mla_columns():
    dqk = MLA_NOPE + MLA_ROPE
    zq, zkv = N_HEADS * dqk, N_HEADS * (MLA_NOPE + MLA_V)
    sw32 = _half_swap(32)
    wq, wqs, wk, wv = [], [], [], []
    for h in range(N_HEADS):
        wq += [h * dqk + np.arange(dqk), np.full((32,), zq)]
        wqs += [np.full((64,), zq), h * dqk + MLA_NOPE + sw32, np.full((32,), zq)]
        wk += [h * 128 + np.arange(64), np.full((64,), zkv)]
        wv += [h * 128 + 64 + np.arange(64), np.full((64,), zkv)]
    return tuple(np.concatenate(c) for c in (wq, wqs, wk, wv))


def _wout_rows():
    z = 4 * GROUP_W
    gg, kk, dd = np.meshgrid(np.arange(2), np.arange(2), np.arange(64), indexing="ij")
    ya = (kk * 128 + gg * 64 + dd).reshape(-1)
    ym = np.concatenate([np.concatenate([2 * GROUP_W + h * 64 + np.arange(64), np.full((64,), z)])
                         for h in range(N_HEADS)])
    return np.concatenate([ya, GROUP_W + np.arange(256), ym, 3 * GROUP_W + np.arange(256)])


def _rope_tables(seq):
    t = jnp.arange(seq)
    row = (t // GRID_W).astype(F32)
    col = (t % GRID_W).astype(F32)

    def cs(n):
        inv = ROPE_THETA ** (-jnp.arange(0, n, 2, dtype=F32) / n)
        ar, ac = row[:, None] * inv[None, :], col[:, None] * inv[None, :]
        cos = jnp.concatenate([jnp.cos(ar)] * 2 + [jnp.cos(ac)] * 2, axis=-1)
        sin = jnp.concatenate([-jnp.sin(ar), jnp.sin(ar), -jnp.sin(ac), jnp.sin(ac)], axis=-1)
        return cos, sin

    cos64, sin64 = cs(HEAD_DIM // 2)
    cos32, sin32 = cs(MLA_ROPE // 2)
    ones = lambda w_: jnp.ones((CTX_LEN, w_), F32)
    zeros = lambda w_: jnp.zeros((CTX_LEN, w_), F32)
    cos_a = jnp.concatenate([ones(128), jnp.concatenate([cos64, cos64], axis=-1)], axis=0)
    sin_a = jnp.concatenate([zeros(128), jnp.concatenate([sin64, sin64], axis=-1)], axis=0)
    pad1, pad0 = jnp.ones((seq, 64), F32), jnp.zeros((seq, 32), F32)
    cos_m = jnp.concatenate([jnp.concatenate([ones(96), zeros(32)], axis=-1),
                             jnp.concatenate([pad1, cos32, pad0], axis=-1)], axis=0)
    sin_m = jnp.concatenate([zeros(128),
                             jnp.concatenate([jnp.zeros((seq, 64), F32), sin32, pad0], axis=-1)], axis=0)
    return cos_a, sin_a, cos_m, sin_m


def _swa_band():
    i = np.arange(4 * SWA_BLOCK)[:, None] % SWA_BLOCK
    j = np.arange(3 * SWA_BLOCK + CTX_LEN)[None, :]
    ok = (j >= 3 * SWA_BLOCK) | ((j >= i) & (j <= i + 2 * SWA_WINDOW))
    return jnp.asarray(np.where(ok, 0.0, NEG_INF), F32)


def _na_bias_table(rpb):
    w = GRID_W
    qc = np.arange(w)[:, None]
    kc = np.arange(w)[None, :]
    cs = np.clip(qc - NA_KC // 2, 0, w - NA_KC)
    ok = (kc >= cs) & (kc < cs + NA_KC)
    dc = np.clip(kc - qc + NA_KC - 1, 0, 2 * NA_KC - 2)
    onehot = (np.arange(2 * NA_KC - 1)[None, :, None] == dc[:, None, :]).astype(np.float32)
    full = jnp.einsum("hrd,qdk->hrqk", rpb, jnp.asarray(onehot), precision=HIGHEST)
    full = jnp.where(jnp.asarray(ok)[None, None], full, NEG_INF)
    tabs = [jnp.transpose(full[:, NA_KR - 1 - dl:2 * NA_KR - 1 - dl], (0, 2, 1, 3)).reshape(N_HEADS * w, NA_KR * w)
            for dl in range(NA_KR)]
    tabs.append(jnp.full((N_HEADS * w, NA_KR * w), NEG_INF, F32))
    return jnp.stack(tabs)


def _dn_constants():
    c = DN_CHUNK
    hh = np.arange(GROUP_W) // HEAD_DIM
    bd = (hh[:, None] == hh[None, :])
    expand = np.zeros((LANES, 4 * GROUP_W), np.float32)
    for piece in range(3):
        for d in range(2):
            expand[16 * piece + d * 4 + hh, d * GROUP_W + np.arange(GROUP_W)] = 1.0
            expand[16 * piece + 8 + d * 4 + hh, (2 + d) * GROUP_W + np.arange(GROUP_W)] = 1.0
    i = np.arange(c)[:, None]
    jj = (np.arange(GROUP_W) % c)[None, :]
    masks = [(jj <= i), (jj < i), (jj >= i), (jj > i), (jj == i)]
    for d in range(2):
        for lvl in range(6):
            b = 1 << lvl
            same = (i // (2 * b)) == (jj // (2 * b))
            lo, hi = (jj % (2 * b)) < b, (i % (2 * b)) >= b
            masks.append(same & (lo & hi if d == 0 else ~lo & ~hi))
    stm = np.stack(masks).astype(np.float32)
    return jnp.asarray(bd, BF16), jnp.asarray(expand, BF16), jnp.asarray(stm)


def kernel(x, c, ctx, c_ctx, ada_w, ada_b, norm1_g, ffn1_wg, ffn1_wu, ffn1_wd, norm2_g, w_in, swa_sink,
           dn_conv_w, dn_a_log, dn_dt_bias, dn_norm_g, mla_q_norm_g, mla_w_uq, mla_kv_norm_g, mla_w_ukv,
           na_rpb, w_out, norm3_g, ffn2_wg, ffn2_wu, ffn2_wd, final_norm_g):
    bsz, seq, d = x.shape
    depth = ada_w.shape[0]
    assert d == D_MODEL and ctx.shape[1] == CTX_LEN and seq % TOKEN_TILE == 0 and bsz <= CTX_ROW

    cvec = jnp.zeros((MOD_ROWS, d), F32).at[:bsz].set(c).at[CTX_ROW].set(c_ctx)
    mod = _modulation(cvec, ada_w, ada_b).reshape(depth, MOD_ROWS, 3, 3, d)

    cos_a, sin_a, cos_m, sin_m = _rope_tables(seq)
    band = _swa_band()
    bdm, expand, stm = _dn_constants()
    in_cols = _inproj_columns()
    mq, mqs, mk, mv = _mla_columns()
    wo_rows = _wout_rows()
    row2 = lambda v: v.reshape(1, -1)

    h = x
    for i in range(depth):
        last = i == depth - 1
        bf = lambda a: a.astype(BF16)
        h = _ffn(h, mod[i, :, 0], row2(norm1_g[i]), bf(ffn1_wg[i]), bf(ffn1_wu[i]), bf(ffn1_wd[i]),
                 ctx=ctx if i == 0 else None)
        wp = bf(_gather_cols(w_in[i], in_cols))
        swa_q, swa_kv, dn_qkv, dn_z, dn_ab, mla_c, na_q, na_kv = _inproj(
            h, mod[i, :, 1], row2(norm2_g[i]), wp, cos_a, sin_a)

        ya = _swa(swa_q, swa_kv, swa_sink[i], band)
        yn = _na(na_q, na_kv, _na_bias_table(na_rpb[i]))
        mq_, mk_, mv_ = _mla_prep(
            mla_c, row2(mla_q_norm_g[i]), row2(mla_kv_norm_g[i]),
            bf(_gather_cols(mla_w_uq[i], mq)), bf(_gather_cols(mla_w_uq[i], mqs)),
            bf(_gather_cols(mla_w_ukv[i], mk)), bf(_gather_cols(mla_w_ukv[i], mv)), cos_m, sin_m)
        ym = _mla_attn(mq_, mk_, mv_)
        pad = lambda v: jnp.zeros((1, LANES), F32).at[0, :2 * N_HEADS].set(v.reshape(-1))
        u, w, qg, kd, intra, egl = _dn_local(dn_qkv, dn_ab, dn_conv_w[i], pad(dn_a_log[i]), pad(dn_dt_bias[i]),
                                             bdm, expand, stm)
        o_f, o_b = _dn_scan(u, w, qg, kd, intra, egl, bdm)

        wo = bf(jnp.take(jnp.concatenate([w_out[i], jnp.zeros((1, d), F32)], axis=0), jnp.asarray(wo_rows), axis=0))
        mix = (ya, o_f, o_b, dn_z, ym, yn, mod[i, :, 1], row2(jnp.tile(dn_norm_g[i], N_HEADS)), bdm, wo)
        h = _ffn(h, mod[i, :, 2], row2(norm3_g[i]), bf(ffn2_wg[i]), bf(ffn2_wu[i]), bf(ffn2_wd[i]), mix=mix,
                 final_g=row2(final_norm_g) if last else None, latent_only=last)
    return h
```

```python
import functools

import numpy as np
import jax
import jax.numpy as jnp
from jax import lax
from jax.experimental import pallas as pl
from jax.experimental.pallas import tpu as pltpu

F32 = jnp.float32
BF16 = jnp.bfloat16
HIGHEST = lax.Precision.HIGHEST

D_MODEL = 1024
CTX_LEN = 256
GRID_W = 64
HEAD_DIM = 64
N_HEADS = 4
GROUP_W = N_HEADS * HEAD_DIM
D_FF = 2816
N_MOD = 9
ROPE_THETA = 10000.0
EPS = 1e-6
NEG_INF = -1e30
LOG2E = 1.4426950408889634
SWA_WINDOW = 128
SWA_BLOCK = 128
DN_CHUNK = 64
DN_LEVEL0 = 5
MLA_Q_RANK = 256
MLA_KV_RANK = 128
MLA_NOPE = 64
MLA_ROPE = 32
MLA_V = 64
NA_KR = 8
NA_KC = 16
NA_UNITS = 4
IN_SIZES = (256, 128, 128, 768, 256, 8, 8, 256, 128, 32, 256, 256, 256)
IN_PROJ = sum(IN_SIZES)

LANES = 128
TOKEN_TILE = 256
VMEM_LIMIT = 56 * 1024 * 1024
MOD_ROWS = 16
CTX_ROW = 8

P_SWA_Q, P_SWA_QS, P_SWA_K, P_SWA_KS, P_SWA_V = 0, 256, 512, 640, 768
P_DN_QKV, P_DN_Z, P_DN_AB = 896, 1664, 1920
P_MLA = 2048
P_NA_Q, P_NA_KV = 2688, 2944
P_TOTAL = 3456


def _cparams(sem):
    return pltpu.CompilerParams(dimension_semantics=sem, vmem_limit_bytes=VMEM_LIMIT)


def _dot(a, b, precision=None):
    return jnp.dot(a, b, preferred_element_type=F32, precision=precision)


def _dot_nt(a, b):
    return lax.dot_general(a, b, (((1,), (1,)), ((), ())), preferred_element_type=F32)


def _dot_tn(a, b):
    return lax.dot_general(a, b, (((0,), (0,)), ((), ())), preferred_element_type=F32)


def _sigmoid(x):
    return 1.0 / (1.0 + jnp.exp(-x))


def _silu(x):
    return x * _sigmoid(x)


def _resident(shape):
    nd = len(shape)
    return pl.BlockSpec(shape, lambda *_: (0,) * nd, pipeline_mode=pl.Buffered(1))


def _mod_kernel(c_ref, w_ref, b_ref, o_ref):
    s = _silu(c_ref[...])
    o_ref[0] = _dot(s, w_ref[0], HIGHEST) + b_ref[0]


def _modulation(cvec, ada_w, ada_b):
    n_layers = ada_w.shape[0]
    d = D_MODEL
    return pl.pallas_call(
        _mod_kernel,
        grid=(n_layers, N_MOD),
        in_specs=[
            pl.BlockSpec((MOD_ROWS, d), lambda l, j: (0, 0)),
            pl.BlockSpec((1, d, d), lambda l, j: (l, 0, j)),
            pl.BlockSpec((1, 1, d), lambda l, j: (l, 0, j)),
        ],
        out_specs=pl.BlockSpec((1, MOD_ROWS, d), lambda l, j: (l, 0, j)),
        out_shape=jax.ShapeDtypeStruct((n_layers, MOD_ROWS, N_MOD * d), F32),
        compiler_params=_cparams(("arbitrary", "arbitrary")),
        name="modulation",
    )(cvec, ada_w, ada_b.reshape(n_layers, 1, N_MOD * d))


def _mod_index(b, i):
    return (jnp.where(i == 0, CTX_ROW, b), 0, 0)


def _prenorm(x, g, shift, scale):
    ms = jnp.mean(x * x, axis=-1, keepdims=True)
    return (x * lax.rsqrt(ms + EPS) * g) * (1.0 + scale) + shift


def _group_mean_sq(x, ones_bd):
    return _dot((x * x).astype(BF16), ones_bd) * (1.0 / HEAD_DIM)


def _ffn_kernel(*refs, mix, final, split):
    it = iter(refs)
    x_ref = next(it)
    if split:
        ctx_ref = next(it)
    if mix:
        ya_ref, of_ref, ob_ref, z_ref, ym_ref, yn_ref = (next(it) for _ in range(6))
        mmod_ref, dng_ref, ones_ref, wo_ref = (next(it) for _ in range(4))
    mod_ref, g_ref, wg_ref, wu_ref, wd_ref = (next(it) for _ in range(5))
    if final:
        fg_ref = next(it)
    o_ref = next(it)

    x = x_ref[0]
    if split:
        x = jnp.where(pl.program_id(1) == 0, ctx_ref[0], x)
    if mix:
        o = of_ref[0] + ob_ref[0]
        yd = (o * lax.rsqrt(_group_mean_sq(o, ones_ref[...]) + EPS) * dng_ref[...]) * _silu(z_ref[0])
        ycat = jnp.concatenate([ya_ref[0], yd.astype(BF16), ym_ref[0], yn_ref[0]], axis=-1)
        x = x + mmod_ref[0][2:3] * _dot(ycat, wo_ref[...])
    mod = mod_ref[0]
    h = _prenorm(x, g_ref[...], mod[0:1], mod[1:2]).astype(BF16)
    a = (_silu(_dot(h, wg_ref[...])) * _dot(h, wu_ref[...])).astype(BF16)
    x = x + (0.5 * mod[2:3]) * _dot(a, wd_ref[...])
    if final:
        ms = jnp.mean(x * x, axis=-1, keepdims=True)
        x = x * lax.rsqrt(ms + EPS) * fg_ref[...]
    o_ref[0] = x


def _ffn(x, mod, g, wg, wu, wd, mix=None, final_g=None, latent_only=False, ctx=None):
    bsz, n, d = x.shape
    tm = TOKEN_TILE
    off = 1 if latent_only else 0
    tiles = n // tm - off + (ctx is not None)
    tok = lambda w: pl.BlockSpec((1, tm, w), lambda b, i: (b, i + off, 0))
    modspec = pl.BlockSpec((1, 3, d), lambda b, i: _mod_index(b, i + off))
    if ctx is not None:
        assert mix is None and not latent_only and ctx.shape[1] == tm
        args = [x, ctx]
        specs = [pl.BlockSpec((1, tm, d), lambda b, i: (b, jnp.maximum(i - 1, 0), 0)),
                 pl.BlockSpec((1, tm, d), lambda b, i: (b, 0, 0))]
    else:
        args, specs = [x], [tok(d)]
    if mix is not None:
        ya, of, ob, z, ym, yn, mmod, dng, ones_bd, wo = mix
        args += [ya, of, ob, z, ym, yn, mmod, dng, ones_bd, wo]
        specs += [tok(ya.shape[-1]), tok(of.shape[-1]), tok(ob.shape[-1]), tok(z.shape[-1]),
                  tok(ym.shape[-1]), tok(yn.shape[-1]), modspec,
                  _resident(dng.shape), _resident(ones_bd.shape), _resident(wo.shape)]
    args += [mod, g, wg, wu, wd]
    specs += [modspec, _resident(g.shape), _resident(wg.shape), _resident(wu.shape), _resident(wd.shape)]
    if final_g is not None:
        args.append(final_g)
        specs.append(_resident(final_g.shape))
    return pl.pallas_call(
        functools.partial(_ffn_kernel, mix=mix is not None, final=final_g is not None, split=ctx is not None),
        grid=(bsz, tiles),
        in_specs=specs,
        out_specs=pl.BlockSpec((1, tm, d), lambda b, i: (b, i, 0)),
        out_shape=jax.ShapeDtypeStruct((bsz, tiles * tm, d), F32),
        compiler_params=_cparams(("parallel", "arbitrary")),
        name="ffn_mix" if mix is not None else "ffn",
    )(*args)


def _inproj_kernel(x_ref, mod_ref, g_ref, w_ref, cos_ref, sin_ref,
                   gq_ref, gkv_ref, wq_ref, wqs_ref, wk_ref, wv_ref, vone_ref, cosm_ref, sinm_ref,
                   swaq_ref, swakv_ref, dnqkv_ref, dnz_ref, dnab_ref, naq_ref, nakv_ref,
                   mq_ref, mk_ref, mv_ref):
    mod = mod_ref[0]
    h = _prenorm(x_ref[0], g_ref[...], mod[0:1], mod[1:2]).astype(BF16)
    u = _dot(h, w_ref[...])
    cos, sin = cos_ref[...], sin_ref[...]
    cos2 = jnp.concatenate([cos, cos], axis=-1)
    sin2 = jnp.concatenate([sin, sin], axis=-1)
    scale = HEAD_DIM ** -0.5 * LOG2E
    swaq_ref[0] = ((u[:, P_SWA_Q:P_SWA_Q + 256] * cos2 + u[:, P_SWA_QS:P_SWA_QS + 256] * sin2) * scale).astype(BF16)
    k = u[:, P_SWA_K:P_SWA_K + 128] * cos + u[:, P_SWA_KS:P_SWA_KS + 128] * sin
    swakv_ref[0] = jnp.concatenate([k, u[:, P_SWA_V:P_SWA_V + 128]], axis=-1).astype(BF16)
    dnqkv_ref[0] = u[:, P_DN_QKV:P_DN_QKV + 768]
    dnz_ref[0] = u[:, P_DN_Z:P_DN_Z + 256]
    dnab_ref[0] = u[:, P_DN_AB:P_DN_AB + 128]
    naq_ref[0] = (u[:, P_NA_Q:P_NA_Q + 256] * scale).astype(BF16)
    nakv_ref[0] = u[:, P_NA_KV:P_NA_KV + 512].astype(BF16)

    cq, ckv = u[:, P_MLA:P_MLA + 256], u[:, P_MLA + 256:P_MLA + 384]
    kra, krb = u[:, P_MLA + 384:P_MLA + 512], u[:, P_MLA + 512:P_MLA + 640]
    nq = (cq * lax.rsqrt(jnp.mean(cq * cq, axis=-1, keepdims=True) + EPS) * gq_ref[...]).astype(BF16)
    nkv = (ckv * lax.rsqrt(jnp.mean(ckv * ckv, axis=-1, keepdims=True) + EPS) * gkv_ref[...]).astype(BF16)
    cosm, sinm = cosm_ref[...], sinm_ref[...]
    cos4 = jnp.concatenate([cosm] * N_HEADS, axis=-1)
    sin4 = jnp.concatenate([sinm] * N_HEADS, axis=-1)
    mscale = (MLA_NOPE + MLA_ROPE) ** -0.5 * LOG2E
    mq_ref[0] = ((_dot(nq, wq_ref[...]) * cos4 + _dot(nq, wqs_ref[...]) * sin4) * mscale).astype(BF16)
    kr = kra * cosm + krb * sinm
    mk_ref[0] = (_dot(nkv, wk_ref[...]) + jnp.concatenate([kr] * N_HEADS, axis=-1)).astype(BF16)
    mv_ref[0] = (_dot(nkv, wv_ref[...]) + vone_ref[...]).astype(BF16)


def _inproj(x, mod, g, w, cos, sin, mla):
    bsz, n, d = x.shape
    tm = TOKEN_TILE
    gq, gkv, wq, wqs, wk, wv, cosm, sinm = mla
    vone = jnp.asarray((np.arange(N_HEADS * LANES) % LANES == MLA_V).astype(np.float32)).reshape(1, -1)
    widths = (256, 256, 768, 256, 128, 256, 512)
    dtypes = (BF16, BF16, F32, F32, F32, BF16, BF16)
    tok = lambda w_: pl.BlockSpec((1, tm, w_), lambda b, i: (b, i, 0))
    tab = pl.BlockSpec((tm, LANES), lambda b, i: (i, 0))
    res = [gq, gkv, wq, wqs, wk, wv, vone]
    mw = N_HEADS * LANES
    return pl.pallas_call(
        _inproj_kernel,
        grid=(bsz, n // tm),
        in_specs=[tok(d), pl.BlockSpec((1, 3, d), _mod_index), _resident(g.shape), _resident(w.shape), tab, tab]
                 + [_resident(a.shape) for a in res] + [tab, tab],
        out_specs=[tok(w_) for w_ in widths] + [tok(mw)] * 3,
        out_shape=[jax.ShapeDtypeStruct((bsz, n, w_), dt) for w_, dt in zip(widths, dtypes)]
                  + [jax.ShapeDtypeStruct((bsz, n, mw), BF16)] * 3,
        compiler_params=_cparams(("parallel", "arbitrary")),
        name="inproj",
    )(x, mod, g, w, cos, sin, *res, cosm, sinm)


def _swa_kernel(sink_ref, q_ref, kvp_ref, kvc_ref, kvn_ref, kvx_ref, band_ref, o_ref):
    i = pl.program_id(1)
    nt = pl.num_programs(1)
    blk = SWA_BLOCK
    lane = lax.broadcasted_iota(jnp.int32, (1, LANES), 1)
    hsel = [lane < HEAD_DIM, lane >= HEAD_DIM]
    hmask = [m.astype(F32) for m in hsel]
    q = q_ref[0]
    cur = kvc_ref[0]
    kvx = kvx_ref[0]
    latent = i > 0
    blocks = [(kvp_ref[0], cur[:blk], cur[blk:]), (cur[:blk], cur[blk:], kvn_ref[0])]
    valid = [(latent & (i > 1), latent, latent), (latent, latent, latent & (i < nt - 1))]
    rblk = lax.broadcasted_iota(jnp.int32, (4 * blk, 1), 0) // blk
    sk = jnp.where(rblk == 0, sink_ref[0], jnp.where(rblk == 1, sink_ref[2],
                                                       jnp.where(rblk == 2, sink_ref[1], sink_ref[3]))) * LOG2E
    zero = jnp.zeros((1, LANES), F32)
    s, vcat = [], []
    for u in range(2):
        qu = q[u * blk:(u + 1) * blk]
        qs = jnp.concatenate([jnp.where(hsel[kh], qu[:, g * LANES:(g + 1) * LANES], 0)
                              for g in range(2) for kh in range(2)], axis=0)
        kvs = list(blocks[u]) + [kvx]
        kcat = jnp.concatenate([t[:, :LANES] for t in kvs], axis=0)
        vcat.append(jnp.concatenate([t[:, LANES:] for t in kvs], axis=0))
        pen = jnp.concatenate([zero + jnp.where(ok, 0.0, NEG_INF) for ok in valid[u]] + [zero, zero], axis=-1)
        s.append(_dot_nt(qs, kcat) + band_ref[...] + pen)
    p, l = [], []
    for u in range(2):
        m = jnp.maximum(jnp.max(s[u], axis=-1, keepdims=True), sk)
        pu = jnp.exp2(s[u] - m)
        l.append(jnp.sum(pu, axis=-1, keepdims=True) + jnp.exp2(sk - m))
        p.append(pu.astype(BF16))
    for u in range(2):
        o = _dot(p[u], vcat[u]) * (1.0 / l[u])
        outs = [o[(2 * g) * blk:(2 * g + 1) * blk] * hmask[0] + o[(2 * g + 1) * blk:(2 * g + 2) * blk] * hmask[1]
                for g in range(2)]
        o_ref[0, u * blk:(u + 1) * blk, :] = jnp.concatenate(outs, axis=-1).astype(o_ref.dtype)


def _swa(q, kv, sink, band):
    bsz, n, _ = q.shape
    blk = SWA_BLOCK
    tm = 2 * blk
    first = CTX_LEN // blk
    last = n // blk - 1
    return pl.pallas_call(
        _swa_kernel,
        grid=(bsz, n // tm),
        in_specs=[
            pl.BlockSpec(memory_space=pltpu.SMEM),
            pl.BlockSpec((1, tm, 256), lambda b, i: (b, i, 0)),
            pl.BlockSpec((1, blk, 256), lambda b, i: (b, jnp.clip(2 * i - 1, first, last), 0)),
            pl.BlockSpec((1, tm, 256), lambda b, i: (b, jnp.maximum(i, 1), 0)),
            pl.BlockSpec((1, blk, 256), lambda b, i: (b, jnp.clip(2 * i + 2, first, last), 0)),
            pl.BlockSpec((1, CTX_LEN, 256), lambda b, i: (b, 0, 0)),
            _resident(band.shape),
        ],
        out_specs=pl.BlockSpec((1, tm, 256), lambda b, i: (b, i, 0)),
        out_shape=jax.ShapeDtypeStruct((bsz, n, 256), BF16),
        compiler_params=_cparams(("parallel", "arbitrary")),
        name="swa",
    )(sink, q, kv, kv, kv, kv, band)


def _na_kernel(q_ref, kv_ref, tab_ref, o_ref, *, rows):
    i = pl.program_id(1)
    w = GRID_W
    nctx = CTX_LEN // w
    nloc = NA_KR * w
    lane = lax.broadcasted_iota(jnp.int32, (1, GROUP_W), 1) // HEAD_DIM
    hsel = [lane == h for h in range(N_HEADS)]
    hmask = [m.astype(F32) for m in hsel]
    kv_ctx = kv_ref[0, pl.ds(0, CTX_LEN), :]
    s, vcat = [], []
    for u in range(NA_UNITS):
        j = NA_UNITS * i + u
        r = jnp.maximum(j - nctx, 0)
        rs = jnp.clip(r - NA_KR // 2, 0, rows - NA_KR)
        tab_i = jnp.where(j < nctx, NA_KR, r - rs)
        qu = q_ref[0, u * w:(u + 1) * w, :]
        qs = jnp.concatenate([jnp.where(hsel[h], qu, 0) for h in range(N_HEADS)], axis=0)
        start = pl.multiple_of(CTX_LEN + rs * w, w)
        kv_loc = kv_ref[0, pl.ds(start, nloc), :]
        kcat = jnp.concatenate([kv_loc[:, :GROUP_W], kv_ctx[:, :GROUP_W]], axis=0)
        vcat.append(jnp.concatenate([kv_loc[:, GROUP_W:], kv_ctx[:, GROUP_W:]], axis=0))
        su = _dot_nt(qs, kcat)
        s.append(jnp.concatenate([su[:, :nloc] + tab_ref[tab_i], su[:, nloc:]], axis=-1))
    p, l = [], []
    for u in range(NA_UNITS):
        pu = jnp.exp2(s[u] - jnp.max(s[u], axis=-1, keepdims=True))
        l.append(jnp.sum(pu, axis=-1, keepdims=True))
        p.append(pu.astype(BF16))
    for u in range(NA_UNITS):
        o = _dot(p[u], vcat[u]) * (1.0 / l[u])
        acc = o[0:w] * hmask[0]
        for h in range(1, N_HEADS):
            acc = acc + o[h * w:(h + 1) * w] * hmask[h]
        o_ref[0, u * w:(u + 1) * w, :] = acc.astype(o_ref.dtype)


def _na(q, kv, tab):
    bsz, n, _ = q.shape
    w = GRID_W
    rows = (n - CTX_LEN) // w
    assert rows >= NA_KR
    return pl.pallas_call(
        functools.partial(_na_kernel, rows=rows),
        grid=(bsz, n // (NA_UNITS * w)),
        in_specs=[
            pl.BlockSpec((1, NA_UNITS * w, 256), lambda b, i: (b, i, 0)),
            pl.BlockSpec((1, n, 512), lambda b, i: (b, 0, 0)),
            _resident(tab.shape),
        ],
        out_specs=pl.BlockSpec((1, NA_UNITS * w, 256), lambda b, i: (b, i, 0)),
        out_shape=jax.ShapeDtypeStruct((bsz, n, 256), BF16),
        compiler_params=_cparams(("parallel", "arbitrary")),
        name="na",
    )(q, kv, tab)


def _mla_attn_kernel(q_ref, k_ref, v_ref, o_ref):
    i = pl.program_id(1)

    def attend(nkeys):
        head = lambda ref, h, rows: ref[0, pl.ds(0, rows), h * LANES:(h + 1) * LANES]
        s, p = {}, {}
        for t in range(N_HEADS + 2):
            if t < N_HEADS:
                s[t] = _dot_nt(head(q_ref, t, TOKEN_TILE), head(k_ref, t, nkeys))
            if 0 <= t - 1 < N_HEADS:
                sh = s.pop(t - 1)
                p[t - 1] = jnp.exp2(sh - jnp.max(sh, axis=-1, keepdims=True)).astype(BF16)
            if 0 <= t - 2 < N_HEADS:
                h = t - 2
                o = _dot(p.pop(h), head(v_ref, h, nkeys))
                o_ref[0, :, h * LANES:(h + 1) * LANES] = (o * (1.0 / o[:, MLA_V:MLA_V + 1])).astype(o_ref.dtype)

    @pl.when(i == 0)
    def _():
        attend(CTX_LEN)

    @pl.when(i > 0)
    def _():
        attend(k_ref.shape[1])


def _mla_attn(q, k, v):
    bsz, n, w = q.shape
    tq = TOKEN_TILE
    return pl.pallas_call(
        _mla_attn_kernel,
        grid=(bsz, n // tq),
        in_specs=[
            pl.BlockSpec((1, tq, w), lambda b, i: (b, i, 0)),
            pl.BlockSpec((1, n, w), lambda b, i: (b, 0, 0)),
            pl.BlockSpec((1, n, w), lambda b, i: (b, 0, 0)),
        ],
        out_specs=pl.BlockSpec((1, tq, w), lambda b, i: (b, i, 0)),
        out_shape=jax.ShapeDtypeStruct((bsz, n, w), BF16),
        compiler_params=_cparams(("parallel", "arbitrary")),
        name="mla_attn",
    )(q, k, v)


def _bd(m, bd_mask):
    return jnp.concatenate([m] * N_HEADS, axis=0) * bd_mask


def _dn_local_kernel(x_ref, xp_ref, xn_ref, ab_ref, cw_ref, alog_ref, dtb_ref, bdm_ref, exp_ref, stm_ref,
                     u_ref, w_ref, qg_ref, kd_ref, in_ref, egl_ref):
    i = pl.program_id(1)
    nt = pl.num_programs(1)
    tm = TOKEN_TILE
    c = DN_CHUNK
    nch = tm // c
    nb = x_ref.shape[0]
    row = lax.broadcasted_iota(jnp.int32, (tm, 1), 0)
    lane = lax.broadcasted_iota(jnp.int32, (1, LANES), 1)
    bdm = bdm_ref[...]
    fronts = [_dn_front(x_ref[bi], xp_ref[bi][7:8], xn_ref[bi][0:1], ab_ref[bi], cw_ref[...], alog_ref[...],
                        dtb_ref[...], bdm, exp_ref[...], row, lane, i, nt) for bi in range(nb)]

    eye = stm_ref[4]
    tile_b = eye.astype(BF16)
    chains = [(bi, d, ci) for bi in range(nb) for ci in range(nch) for d in range(2)]
    rows = lambda ci: slice(ci * c, (ci + 1) * c)
    kq = {}
    for bi in range(nb):
        q, k, _, _, bexps = fronts[bi]
        for ci in range(nch):
            kc = k[rows(ci)]
            kt_bd = _dot_tn(kc.astype(BF16), tile_b).astype(BF16) * bdm
            lhs = jnp.concatenate([kc * bexps[0][rows(ci)], kc * bexps[1][rows(ci)], q[rows(ci)]], axis=0)
            kq[bi, ci] = _dot(lhs.astype(BF16), kt_bd)
    lbs, xinvs = {}, {}
    for bi, d, ci in chains:
        q, k, _, gcols, _ = fronts[bi]
        sl = rows(ci)
        gcol = gcols[d][sl]
        grow = jnp.sum(gcol * eye, axis=0, keepdims=True)
        incl, strict = stm_ref[2 * d], stm_ref[2 * d + 1]
        dec_incl = jnp.exp((gcol - grow) * incl) * incl
        lmat = kq[bi, ci][d * c:(d + 1) * c] * (dec_incl * strict)
        egc = jnp.exp(gcol)
        glast = gcol[c - 1:c] if d == 0 else gcol[0:1]
        in_ref[d, bi, sl, :] = (kq[bi, ci][2 * c:3 * c] * dec_incl).astype(BF16)
        qg_ref[d, bi, sl, :] = (q[sl] * egc).astype(BF16)
        kd_ref[d, bi, sl, :] = (k[sl] * jnp.exp(glast - gcol)).astype(BF16)
        egl_ref[d, bi, ci * 8:(ci + 1) * 8, :] = jnp.broadcast_to(jnp.exp(glast), (8, GROUP_W))
        lbs[bi, d, ci] = lmat.astype(BF16)
        xinvs[bi, d, ci] = eye - lmat * stm_ref[DN_LEVEL0 + 6 * d]
    for lvl in range(1, 6):
        ys = {}
        for ch in chains:
            off = _bd(lbs[ch] * stm_ref[DN_LEVEL0 + 6 * ch[1] + lvl].astype(BF16), bdm)
            ys[ch] = _dot(xinvs[ch].astype(BF16), off)
        for ch in chains:
            xinvs[ch] = xinvs[ch] - _dot(ys[ch].astype(BF16), _bd(xinvs[ch].astype(BF16), bdm))
    for bi, d, ci in chains:
        _, k, v, gcols, bexps = fronts[bi]
        sl = rows(ci)
        xb = xinvs[bi, d, ci].astype(BF16)
        kb = k[sl] * bexps[d][sl]
        u_ref[d, bi, sl, :] = _dot(xb, _bd((v[sl] * bexps[d][sl]).astype(BF16), bdm))
        w_ref[d, bi, sl, :] = _dot(xb, _bd((kb * jnp.exp(gcols[d][sl])).astype(BF16), bdm)).astype(BF16)


def _dn_front(x, prev_row, next_row, ab, cw, alog, dtb, bdm, expand, row, lane, i, nt):
    tm = TOKEN_TILE
    c = DN_CHUNK
    prev_row = prev_row * (i > 1).astype(F32)
    next_row = next_row * ((i > 0) & (i < nt - 1)).astype(F32)
    xm1 = jnp.where(row == 0, prev_row, pltpu.roll(x, 1, 0))
    xp1 = jnp.where(row == tm - 1, next_row, pltpu.roll(x, tm - 1, 0))
    h = _silu(cw[0:1] * xm1 + cw[1:2] * x + cw[2:3] * xp1)
    q, k, v = h[:, 0:256], h[:, 256:512], h[:, 512:768]
    q = q * lax.rsqrt(_dot((q * q).astype(BF16), bdm) + EPS) * (HEAD_DIM ** -0.5)
    k = k * lax.rsqrt(_dot((k * k).astype(BF16), bdm) + EPS)

    z = ab + dtb
    softplus = jnp.maximum(z, 0.0) + jnp.log(1.0 + jnp.exp(-jnp.abs(z)))
    t = jnp.where(lane < 2 * N_HEADS, -jnp.exp(alog) * softplus, _sigmoid(ab))
    t = jnp.where(lane < 4 * N_HEADS, t, 0.0)
    rowc = row % c
    pre, suf = t, t
    step = 1
    while step < c:
        pre = pre + jnp.where(rowc >= step, pltpu.roll(pre, step, 0), 0.0)
        suf = suf + jnp.where(rowc < c - step, pltpu.roll(suf, tm - step, 0), 0.0)
        step *= 2
    t = jnp.where(lane < N_HEADS, pre, jnp.where(lane < 2 * N_HEADS, suf, t))
    hi = t.astype(BF16).astype(F32)
    rem = t - hi
    mid = rem.astype(BF16).astype(F32)
    comb = (hi + pltpu.roll(mid, 4 * N_HEADS, 1) + pltpu.roll(rem - mid, 8 * N_HEADS, 1)).astype(BF16)
    ex = _dot(comb, expand)
    return q, k, v, [ex[:, 0:256], ex[:, 256:512]], [ex[:, 512:768], ex[:, 768:1024]]


def _dn_local(qkv, ab, cw, alog, dtb, bdm, expand, stm):
    bsz, n, _ = qkv.shape
    tm = TOKEN_TILE
    nt = n // tm
    r8 = tm // 8
    nb = 2 if bsz % 2 == 0 else 1
    tok = lambda w_: pl.BlockSpec((nb, tm, w_), lambda b, i: (b, i, 0))
    dtok = lambda: pl.BlockSpec((2, nb, tm, 256), lambda b, i: (0, b, i, 0))
    outs = [jax.ShapeDtypeStruct((2, bsz, n, 256), dt) for dt in (F32, BF16, BF16, BF16, BF16)]
    outs.append(jax.ShapeDtypeStruct((2, bsz, n // 8, 256), F32))
    return pl.pallas_call(
        _dn_local_kernel,
        grid=(bsz // nb, nt),
        in_specs=[
            tok(768),
            pl.BlockSpec((nb, 8, 768), lambda b, i: (b, jnp.maximum(i * r8 - 1, 0), 0)),
            pl.BlockSpec((nb, 8, 768), lambda b, i: (b, jnp.minimum((i + 1) * r8, nt * r8 - 1), 0)),
            tok(128),
        ] + [_resident(a.shape) for a in (cw, alog, dtb, bdm, expand, stm)],
        out_specs=[dtok() for _ in range(5)] + [pl.BlockSpec((2, nb, tm // 8, 256), lambda b, i: (0, b, i, 0))],
        out_shape=outs,
        compiler_params=_cparams(("parallel", "arbitrary")),
        name="dn_local",
    )(qkv, qkv, qkv, ab, cw, alog, dtb, bdm, expand, stm)


def _dn_scan_kernel(uf_ref, wf_ref, qf_ref, kf_ref, if_ref, ef_ref,
                    ub_ref, wb_ref, qb_ref, kb_ref, ib_ref, eb_ref, bdm_ref,
                    of_ref, ob_ref, s_ref):
    c = DN_CHUNK
    nch = TOKEN_TILE // c
    nb = of_ref.shape[0]

    @pl.when(pl.program_id(1) == 0)
    def _():
        s_ref[...] = jnp.zeros_like(s_ref)

    bdm = bdm_ref[...]
    bdm_f = bdm.astype(F32)
    ins = ((uf_ref, wf_ref, qf_ref, kf_ref, if_ref, ef_ref), (ub_ref, wb_ref, qb_ref, kb_ref, ib_ref, eb_ref))
    outs = (of_ref, ob_ref)
    for step in range(nch):
        chains = [(d, bi, step if d == 0 else nch - 1 - step) for bi in range(nb) for d in range(2)]
        rows = lambda ci: slice(ci * c, (ci + 1) * c)
        ws, vb = {}, {}
        for d, bi, ci in chains:
            _, w_ref, q_ref, _, _, _ = ins[d]
            lhs = jnp.concatenate([w_ref[0, bi, rows(ci), :], q_ref[0, bi, rows(ci), :]], axis=0)
            ws[d, bi] = _dot(lhs, s_ref[d, bi].astype(BF16))
        for d, bi, ci in chains:
            u_ref, _, _, _, i_ref, _ = ins[d]
            vb[d, bi] = (u_ref[0, bi, rows(ci), :] - ws[d, bi][:c]).astype(BF16)
            outs[d][bi, rows(ci), :] = ws[d, bi][c:] + _dot(i_ref[0, bi, rows(ci), :], _bd(vb[d, bi], bdm))
        for d, bi, ci in chains:
            _, _, _, k_ref, _, e_ref = ins[d]
            egl = e_ref[0, bi, ci * 8:ci * 8 + 1, :]
            s_ref[d, bi] = s_ref[d, bi] * egl + _dot_tn(k_ref[0, bi, rows(ci), :], vb[d, bi]) * bdm_f


def _dn_scan(u, w, qg, kd, intra, egl, bdm):
    _, bsz, n, _ = u.shape
    tm = TOKEN_TILE
    nt = n // tm
    nb = 4 if bsz % 4 == 0 else (2 if bsz % 2 == 0 else 1)
    bidx = lambda s: jnp.where(s == 0, 0, nt - s)
    f4 = lambda w_: pl.BlockSpec((1, nb, w_, 256), lambda b, s: (0, b, s, 0))
    b4 = lambda w_: pl.BlockSpec((1, nb, w_, 256), lambda b, s: (1, b, bidx(s), 0))
    arrs = (u, w, qg, kd, intra)
    return pl.pallas_call(
        _dn_scan_kernel,
        grid=(bsz // nb, nt),
        in_specs=[f4(tm)] * 5 + [f4(tm // 8)] + [b4(tm)] * 5 + [b4(tm // 8)] + [_resident(bdm.shape)],
        out_specs=[pl.BlockSpec((nb, tm, 256), lambda b, s: (b, s, 0)),
                   pl.BlockSpec((nb, tm, 256), lambda b, s: (b, bidx(s), 0))],
        out_shape=[jax.ShapeDtypeStruct((bsz, n, 256), F32)] * 2,
        scratch_shapes=[pltpu.VMEM((2, nb, GROUP_W, GROUP_W), F32)],
        compiler_params=_cparams(("parallel", "arbitrary")),
        name="dn_scan",
    )(*arrs, egl, *arrs, egl, bdm)


def _half_swap(n):
    q = n // 4
    i = np.arange(n)
    return np.where((i % (2 * q)) < q, i + q, i - q)


def _inproj_columns():
    off = np.cumsum((0,) + IN_SIZES)
    z = IN_PROJ
    sw64, sw32 = _half_swap(64), _half_swap(32)
    gg, kk, dd = np.meshgrid(np.arange(2), np.arange(2), np.arange(64), indexing="ij")
    swa_q = (off[0] + kk * 128 + gg * 64 + dd).reshape(-1)
    swa_qs = (off[0] + kk * 128 + gg * 64 + sw64[dd]).reshape(-1)
    kk2, dd2 = np.meshgrid(np.arange(2), np.arange(64), indexing="ij")
    swa_k = (off[1] + kk2 * 64 + dd2).reshape(-1)
    swa_ks = (off[1] + kk2 * 64 + sw64[dd2]).reshape(-1)
    zeros = lambda n: np.full((n,), z)
    cols = np.concatenate([
        swa_q, swa_qs, swa_k, swa_ks, off[2] + np.arange(128),
        off[3] + np.arange(768), off[4] + np.arange(256),
        off[5] + np.arange(8), off[6] + np.arange(8), zeros(112),
        off[7] + np.arange(256), off[8] + np.arange(128),
        zeros(64), off[9] + np.arange(32), zeros(32),
        zeros(64), off[9] + sw32, zeros(32),
        off[10] + np.arange(256), off[11] + np.arange(256), off[12] + np.arange(256),
    ])
    assert cols.shape == (P_TOTAL,)
    return cols


def _gather_cols(w, cols):
    wz = jnp.concatenate([w, jnp.zeros(w.shape[:-1] + (1,), w.dtype)], axis=-1)
    return jnp.take(wz, jnp.asarray(cols), axis=-1)


def _mla_columns():
    dqk = MLA_NOPE + MLA_ROPE
    zq, zkv = N_HEADS * dqk, N_HEADS * (MLA_NOPE + MLA_V)
    sw32 = _half_swap(32)
    wq, wqs, wk, wv = [], [], [], []
    for h in range(N_HEADS):
        wq += [h * dqk + np.arange(dqk), np.full((32,), zq)]
        wqs += [np.full((64,), zq), h * dqk + MLA_NOPE + sw32, np.full((32,), zq)]
        wk += [h * 128 + np.arange(64), np.full((64,), zkv)]
        wv += [h * 128 + 64 + np.arange(64), np.full((64,), zkv)]
    return tuple(np.concatenate(c) for c in (wq, wqs, wk, wv))


def _wout_rows():
    z = 4 * GROUP_W
    gg, kk, dd = np.meshgrid(np.arange(2), np.arange(2), np.arange(64), indexing="ij")
    ya = (kk * 128 + gg * 64 + dd).reshape(-1)
    ym = np.concatenate([np.concatenate([2 * GROUP_W + h * 64 + np.arange(64), np.full((64,), z)])
                         for h in range(N_HEADS)])
    return np.concatenate([ya, GROUP_W + np.arange(256), ym, 3 * GROUP_W + np.arange(256)])


def _rope_tables(seq):
    t = np.arange(seq)
    row = (t // GRID_W).astype(np.float32)
    col = (t % GRID_W).astype(np.float32)

    def cs(n):
        inv = (np.float32(ROPE_THETA) ** (-np.arange(0, n, 2, dtype=np.float32) / np.float32(n))).astype(np.float32)
        ar, ac = row[:, None] * inv[None, :], col[:, None] * inv[None, :]
        cos = np.concatenate([np.cos(ar)] * 2 + [np.cos(ac)] * 2, axis=-1)
        sin = np.concatenate([-np.sin(ar), np.sin(ar), -np.sin(ac), np.sin(ac)], axis=-1)
        return cos.astype(np.float32), sin.astype(np.float32)

    cos64, sin64 = cs(HEAD_DIM // 2)
    cos32, sin32 = cs(MLA_ROPE // 2)
    ones = lambda w_: np.ones((CTX_LEN, w_), np.float32)
    zeros = lambda w_: np.zeros((CTX_LEN, w_), np.float32)
    cos_a = np.concatenate([ones(128), np.concatenate([cos64, cos64], axis=-1)], axis=0)
    sin_a = np.concatenate([zeros(128), np.concatenate([sin64, sin64], axis=-1)], axis=0)
    pad1, pad0 = np.ones((seq, 64), np.float32), np.zeros((seq, 32), np.float32)
    cos_m = np.concatenate([np.concatenate([ones(96), zeros(32)], axis=-1),
                            np.concatenate([pad1, cos32, pad0], axis=-1)], axis=0)
    sin_m = np.concatenate([zeros(128),
                            np.concatenate([np.zeros((seq, 64), np.float32), sin32, pad0], axis=-1)], axis=0)
    return tuple(jnp.asarray(a) for a in (cos_a, sin_a, cos_m, sin_m))


def _swa_band():
    i = np.arange(4 * SWA_BLOCK)[:, None] % SWA_BLOCK
    j = np.arange(3 * SWA_BLOCK + CTX_LEN)[None, :]
    ok = (j >= 3 * SWA_BLOCK) | ((j >= i) & (j <= i + 2 * SWA_WINDOW))
    return jnp.asarray(np.where(ok, 0.0, NEG_INF), F32)


def _na_bias_table(rpb):
    w = GRID_W
    qc = np.arange(w)[:, None]
    kc = np.arange(w)[None, :]
    cs = np.clip(qc - NA_KC // 2, 0, w - NA_KC)
    ok = (kc >= cs) & (kc < cs + NA_KC)
    dc = np.clip(kc - qc + NA_KC - 1, 0, 2 * NA_KC - 2)
    onehot = (np.arange(2 * NA_KC - 1)[None, :, None] == dc[:, None, :]).astype(np.float32)
    full = jnp.einsum("hrd,qdk->hrqk", rpb, jnp.asarray(onehot), precision=HIGHEST)
    full = jnp.where(jnp.asarray(ok)[None, None], full, NEG_INF)
    tabs = [jnp.transpose(full[:, NA_KR - 1 - dl:2 * NA_KR - 1 - dl], (0, 2, 1, 3)).reshape(N_HEADS * w, NA_KR * w)
            for dl in range(NA_KR)]
    tabs.append(jnp.full((N_HEADS * w, NA_KR * w), NEG_INF, F32))
    return jnp.stack(tabs)


def _dn_constants():
    c = DN_CHUNK
    hh = np.arange(GROUP_W) // HEAD_DIM
    bd = (hh[:, None] == hh[None, :])
    expand = np.zeros((LANES, 4 * GROUP_W), np.float32)
    for piece in range(3):
        for d in range(2):
            expand[16 * piece + d * 4 + hh, d * GROUP_W + np.arange(GROUP_W)] = 1.0
            expand[16 * piece + 8 + d * 4 + hh, (2 + d) * GROUP_W + np.arange(GROUP_W)] = 1.0
    i = np.arange(c)[:, None]
    jj = (np.arange(GROUP_W) % c)[None, :]
    masks = [(jj <= i), (jj < i), (jj >= i), (jj > i), (jj == i)]
    for d in range(2):
        for lvl in range(6):
            b = 1 << lvl
            same = (i // (2 * b)) == (jj // (2 * b))
            lo, hi = (jj % (2 * b)) < b, (i % (2 * b)) >= b
            masks.append(same & (lo & hi if d == 0 else ~lo & ~hi))
    stm = np.stack(masks).astype(np.float32)
    return jnp.asarray(bd, BF16), jnp.asarray(expand, BF16), jnp.asarray(stm)


def kernel(x, c, ctx, c_ctx, ada_w, ada_b, norm1_g, ffn1_wg, ffn1_wu, ffn1_wd, norm2_g, w_in, swa_sink,
           dn_conv_w, dn_a_log, dn_dt_bias, dn_norm_g, mla_q_norm_g, mla_w_uq, mla_kv_norm_g, mla_w_ukv,
           na_rpb, w_out, norm3_g, ffn2_wg, ffn2_wu, ffn2_wd, final_norm_g):
    bsz, seq, d = x.shape
    depth = ada_w.shape[0]
    assert d == D_MODEL and ctx.shape[1] == CTX_LEN and seq % TOKEN_TILE == 0 and bsz <= CTX_ROW

    cvec = jnp.zeros((MOD_ROWS, d), F32).at[:bsz].set(c).at[CTX_ROW].set(c_ctx)
    mod = _modulation(cvec, ada_w, ada_b).reshape(depth, MOD_ROWS, 3, 3, d)

    cos_a, sin_a, cos_m, sin_m = _rope_tables(seq)
    band = _swa_band()
    bdm, expand, stm = _dn_constants()
    in_cols = _inproj_columns()
    mq, mqs, mk, mv = _mla_columns()
    wo_rows = _wout_rows()
    row2 = lambda v: v.reshape(1, -1)

    h = x
    for i in range(depth):
        last = i == depth - 1
        bf = lambda a: a.astype(BF16)
        h = _ffn(h, mod[i, :, 0], row2(norm1_g[i]), bf(ffn1_wg[i]), bf(ffn1_wu[i]), bf(ffn1_wd[i]),
                 ctx=ctx if i == 0 else None)
        wp = bf(_gather_cols(w_in[i], in_cols))
        mla = (row2(mla_q_norm_g[i]), row2(mla_kv_norm_g[i]),
               bf(_gather_cols(mla_w_uq[i], mq)), bf(_gather_cols(mla_w_uq[i], mqs)),
               bf(_gather_cols(mla_w_ukv[i], mk)), bf(_gather_cols(mla_w_ukv[i], mv)), cos_m, sin_m)
        swa_q, swa_kv, dn_qkv, dn_z, dn_ab, na_q, na_kv, mla_q, mla_k, mla_v = _inproj(
            h, mod[i, :, 1], row2(norm2_g[i]), wp, cos_a, sin_a, mla)

        ya = _swa(swa_q, swa_kv, swa_sink[i], band)
        yn = _na(na_q, na_kv, _na_bias_table(na_rpb[i] * LOG2E))
        ym = _mla_attn(mla_q, mla_k, mla_v)
        pad = lambda v: jnp.zeros((1, LANES), F32).at[0, :2 * N_HEADS].set(v.reshape(-1))
        u, w, qg, kd, intra, egl = _dn_local(dn_qkv, dn_ab, dn_conv_w[i], pad(dn_a_log[i]), pad(dn_dt_bias[i]),
                                             bdm, expand, stm)
        o_f, o_b = _dn_scan(u, w, qg, kd, intra, egl, bdm)

        wo = bf(jnp.take(jnp.concatenate([w_out[i], jnp.zeros((1, d), F32)], axis=0), jnp.asarray(wo_rows), axis=0))
        mix = (ya, o_f, o_b, dn_z, ym, yn, mod[i, :, 1], row2(jnp.tile(dn_norm_g[i], N_HEADS)), bdm, wo)
        h = _ffn(h, mod[i, :, 2], row2(norm3_g[i]), bf(ffn2_wg[i]), bf(ffn2_wu[i]), bf(ffn2_wd[i]), mix=mix,
                 final_g=row2(final_norm_g) if last else None, latent_only=last)
    return h
```

```python
import functools

import numpy as np
import jax
import jax.numpy as jnp
from jax import lax
from jax.experimental import pallas as pl
from jax.experimental.pallas import tpu as pltpu

F32 = jnp.float32
BF16 = jnp.bfloat16
HIGHEST = lax.Precision.HIGHEST

D_MODEL = 1024
CTX_LEN = 256
GRID_W = 64
HEAD_DIM = 64
N_HEADS = 4
GROUP_W = N_HEADS * HEAD_DIM
D_FF = 2816
N_MOD = 9
ROPE_THETA = 10000.0
EPS = 1e-6
NEG_INF = -1e30
LOG2E = 1.4426950408889634
SWA_WINDOW = 128
SWA_BLOCK = 128
DN_CHUNK = 64
DN_LEVEL0 = 5
PK_U, PK_W, PK_QG, PK_KD, PK_IN = (slice(j * 256, (j + 1) * 256) for j in range(5))
PK_WIDTH = 5 * 256
MLA_Q_RANK = 256
MLA_KV_RANK = 128
MLA_NOPE = 64
MLA_ROPE = 32
MLA_V = 64
NA_KR = 8
NA_KC = 16
NA_UNITS = 4
IN_SIZES = (256, 128, 128, 768, 256, 8, 8, 256, 128, 32, 256, 256, 256)
IN_PROJ = sum(IN_SIZES)

LANES = 128
TOKEN_TILE = 256
VMEM_LIMIT = 56 * 1024 * 1024
MOD_ROWS = 16
CTX_ROW = 8
BATCH_ROWS = 2

P_SWA_Q, P_SWA_QS, P_SWA_K, P_SWA_KS, P_SWA_V = 0, 256, 512, 640, 768
P_DN_QKV, P_DN_Z, P_DN_AB = 896, 1664, 1920
P_MLA = 2048
P_NA_Q, P_NA_KV = 2688, 2944
P_TOTAL = 3456


def _cparams(sem):
    return pltpu.CompilerParams(dimension_semantics=sem, vmem_limit_bytes=VMEM_LIMIT)


def _dot(a, b, precision=None):
    return jnp.dot(a, b, preferred_element_type=F32, precision=precision)


def _dot_nt(a, b):
    return lax.dot_general(a, b, (((1,), (1,)), ((), ())), preferred_element_type=F32)


def _dot_tn(a, b):
    return lax.dot_general(a, b, (((0,), (0,)), ((), ())), preferred_element_type=F32)


def _sigmoid(x):
    return 1.0 / (1.0 + jnp.exp(-x))


def _silu(x):
    return x * _sigmoid(x)


def _resident(shape):
    nd = len(shape)
    return pl.BlockSpec(shape, lambda *_: (0,) * nd, pipeline_mode=pl.Buffered(1))


def _mod_kernel(c_ref, w_ref, b_ref, o_ref):
    s = _silu(c_ref[...])
    o_ref[0] = _dot(s, w_ref[0], HIGHEST) + b_ref[0]


def _modulation(cvec, ada_w, ada_b):
    n_layers = ada_w.shape[0]
    d = D_MODEL
    return pl.pallas_call(
        _mod_kernel,
        grid=(n_layers, N_MOD),
        in_specs=[
            pl.BlockSpec((MOD_ROWS, d), lambda l, j: (0, 0)),
            pl.BlockSpec((1, d, d), lambda l, j: (l, 0, j)),
            pl.BlockSpec((1, 1, d), lambda l, j: (l, 0, j)),
        ],
        out_specs=pl.BlockSpec((1, MOD_ROWS, d), lambda l, j: (l, 0, j)),
        out_shape=jax.ShapeDtypeStruct((n_layers, MOD_ROWS, N_MOD * d), F32),
        compiler_params=_cparams(("arbitrary", "arbitrary")),
        name="modulation",
    )(cvec, ada_w, ada_b.reshape(n_layers, 1, N_MOD * d))


def _mod_index(b, i, nb=1):
    return (jnp.where(i == 0, CTX_ROW // nb, b), 0, 0)


def _prenorm(x, g, shift, scale):
    ms = jnp.mean(x * x, axis=-1, keepdims=True)
    return (x * lax.rsqrt(ms + EPS) * g) * (1.0 + scale) + shift


def _group_mean_sq(x, ones_bd):
    return _dot((x * x).astype(BF16), ones_bd) * (1.0 / HEAD_DIM)


def _ffn_kernel(*refs, mix, final, split):
    it = iter(refs)
    x_ref = next(it)
    if split:
        ctx_ref = next(it)
    if mix:
        ya_ref, of_ref, ob_ref, z_ref, ym_ref, yn_ref = (next(it) for _ in range(6))
        mmod_ref, dng_ref, ones_ref, wo_ref = (next(it) for _ in range(4))
    mod_ref, g_ref, wg_ref, wu_ref, wd_ref = (next(it) for _ in range(5))
    if final:
        fg_ref = next(it)
    o_ref = next(it)

    nb, tm, _ = x_ref.shape
    xs, hs = [], []
    for bi in range(nb):
        x = x_ref[bi]
        if split:
            x = jnp.where(pl.program_id(1) == 0, ctx_ref[bi], x)
        if mix:
            o = of_ref[bi] + ob_ref[bi]
            yd = (o * lax.rsqrt(_group_mean_sq(o, ones_ref[...]) + EPS) * dng_ref[...]) * _silu(z_ref[bi])
            ycat = jnp.concatenate([ya_ref[bi], yd.astype(BF16), ym_ref[bi], yn_ref[bi]], axis=-1)
            x = x + mmod_ref[bi][2:3] * _dot(ycat, wo_ref[...])
        mod = mod_ref[bi]
        xs.append(x)
        hs.append(_prenorm(x, g_ref[...], mod[0:1], mod[1:2]).astype(BF16))
    h = jnp.concatenate(hs, axis=0)
    a = (_silu(_dot(h, wg_ref[...])) * _dot(h, wu_ref[...])).astype(BF16)
    y = _dot(a, wd_ref[...])
    for bi in range(nb):
        x = xs[bi] + (0.5 * mod_ref[bi][2:3]) * y[bi * tm:(bi + 1) * tm]
        if final:
            ms = jnp.mean(x * x, axis=-1, keepdims=True)
            x = x * lax.rsqrt(ms + EPS) * fg_ref[...]
        o_ref[bi] = x


def _ffn(x, mod, g, wg, wu, wd, mix=None, final_g=None, latent_only=False, ctx=None):
    bsz, n, d = x.shape
    tm = TOKEN_TILE
    nb = BATCH_ROWS if bsz % BATCH_ROWS == 0 else 1
    off = 1 if latent_only else 0
    tiles = n // tm - off + (ctx is not None)
    tok = lambda w: pl.BlockSpec((nb, tm, w), lambda b, i: (b, i + off, 0))
    modspec = pl.BlockSpec((nb, 3, d), lambda b, i: _mod_index(b, i + off, nb))
    if ctx is not None:
        assert mix is None and not latent_only and ctx.shape[1] == tm
        args = [x, ctx]
        specs = [pl.BlockSpec((nb, tm, d), lambda b, i: (b, jnp.maximum(i - 1, 0), 0)),
                 pl.BlockSpec((nb, tm, d), lambda b, i: (b, 0, 0))]
    else:
        args, specs = [x], [tok(d)]
    if mix is not None:
        ya, of, ob, z, ym, yn, mmod, dng, ones_bd, wo = mix
        args += [ya, of, ob, z, ym, yn, mmod, dng, ones_bd, wo]
        specs += [tok(ya.shape[-1]), tok(of.shape[-1]), tok(ob.shape[-1]), tok(z.shape[-1]),
                  tok(ym.shape[-1]), tok(yn.shape[-1]), modspec,
                  _resident(dng.shape), _resident(ones_bd.shape), _resident(wo.shape)]
    args += [mod, g, wg, wu, wd]
    specs += [modspec, _resident(g.shape), _resident(wg.shape), _resident(wu.shape), _resident(wd.shape)]
    if final_g is not None:
        args.append(final_g)
        specs.append(_resident(final_g.shape))
    return pl.pallas_call(
        functools.partial(_ffn_kernel, mix=mix is not None, final=final_g is not None, split=ctx is not None),
        grid=(bsz // nb, tiles),
        in_specs=specs,
        out_specs=pl.BlockSpec((nb, tm, d), lambda b, i: (b, i, 0)),
        out_shape=jax.ShapeDtypeStruct((bsz, tiles * tm, d), F32),
        compiler_params=_cparams(("parallel", "arbitrary")),
        name="ffn_mix" if mix is not None else "ffn",
    )(*args)


def _inproj_kernel(x_ref, mod_ref, g_ref, w_ref, cos_ref, sin_ref,
                   gq_ref, gkv_ref, wq_ref, wqs_ref, wk_ref, wv_ref, vone_ref, cosm_ref, sinm_ref,
                   swaq_ref, swakv_ref, dnqkv_ref, dnz_ref, dnab_ref, naq_ref, nakv_ref,
                   mq_ref, mk_ref, mv_ref):
    nb, tm, _ = x_ref.shape
    g = g_ref[...]
    h = jnp.concatenate([_prenorm(x_ref[bi], g, mod_ref[bi][0:1], mod_ref[bi][1:2]).astype(BF16)
                         for bi in range(nb)], axis=0)
    u_all = _dot(h, w_ref[...])
    cos, sin = cos_ref[...], sin_ref[...]
    cos2 = jnp.concatenate([cos, cos], axis=-1)
    sin2 = jnp.concatenate([sin, sin], axis=-1)
    scale = HEAD_DIM ** -0.5 * LOG2E
    cosm, sinm = cosm_ref[...], sinm_ref[...]
    cos4 = jnp.concatenate([cosm] * N_HEADS, axis=-1)
    sin4 = jnp.concatenate([sinm] * N_HEADS, axis=-1)
    mscale = (MLA_NOPE + MLA_ROPE) ** -0.5 * LOG2E
    for bi in range(nb):
        u = u_all[bi * tm:(bi + 1) * tm]
        swaq_ref[bi] = ((u[:, P_SWA_Q:P_SWA_Q + 256] * cos2 + u[:, P_SWA_QS:P_SWA_QS + 256] * sin2) * scale).astype(BF16)
        k = u[:, P_SWA_K:P_SWA_K + 128] * cos + u[:, P_SWA_KS:P_SWA_KS + 128] * sin
        swakv_ref[bi] = jnp.concatenate([k, u[:, P_SWA_V:P_SWA_V + 128]], axis=-1).astype(BF16)
        dnqkv_ref[bi] = u[:, P_DN_QKV:P_DN_QKV + 768]
        dnz_ref[bi] = u[:, P_DN_Z:P_DN_Z + 256]
        dnab_ref[bi] = u[:, P_DN_AB:P_DN_AB + 128]
        naq_ref[bi] = (u[:, P_NA_Q:P_NA_Q + 256] * scale).astype(BF16)
        nakv_ref[bi] = u[:, P_NA_KV:P_NA_KV + 512].astype(BF16)

        cq, ckv = u[:, P_MLA:P_MLA + 256], u[:, P_MLA + 256:P_MLA + 384]
        kra, krb = u[:, P_MLA + 384:P_MLA + 512], u[:, P_MLA + 512:P_MLA + 640]
        nq = (cq * lax.rsqrt(jnp.mean(cq * cq, axis=-1, keepdims=True) + EPS) * gq_ref[...]).astype(BF16)
        nkv = (ckv * lax.rsqrt(jnp.mean(ckv * ckv, axis=-1, keepdims=True) + EPS) * gkv_ref[...]).astype(BF16)
        mq_ref[bi] = ((_dot(nq, wq_ref[...]) * cos4 + _dot(nq, wqs_ref[...]) * sin4) * mscale).astype(BF16)
        kr = kra * cosm + krb * sinm
        mk_ref[bi] = (_dot(nkv, wk_ref[...]) + jnp.concatenate([kr] * N_HEADS, axis=-1)).astype(BF16)
        mv_ref[bi] = (_dot(nkv, wv_ref[...]) + vone_ref[...]).astype(BF16)


def _inproj(x, mod, g, w, cos, sin, mla):
    bsz, n, d = x.shape
    tm = TOKEN_TILE
    gq, gkv, wq, wqs, wk, wv, cosm, sinm = mla
    vone = jnp.asarray((np.arange(N_HEADS * LANES) % LANES == MLA_V).astype(np.float32)).reshape(1, -1)
    widths = (256, 256, 768, 256, 128, 256, 512)
    dtypes = (BF16, BF16, F32, F32, F32, BF16, BF16)
    nb = BATCH_ROWS if bsz % BATCH_ROWS == 0 else 1
    tok = lambda w_: pl.BlockSpec((nb, tm, w_), lambda b, i: (b, i, 0))
    tab = pl.BlockSpec((tm, LANES), lambda b, i: (i, 0))
    res = [gq, gkv, wq, wqs, wk, wv, vone]
    mw = N_HEADS * LANES
    return pl.pallas_call(
        _inproj_kernel,
        grid=(bsz // nb, n // tm),
        in_specs=[tok(d), pl.BlockSpec((nb, 3, d), lambda b, i: _mod_index(b, i, nb)), _resident(g.shape),
                  _resident(w.shape), tab, tab]
                 + [_resident(a.shape) for a in res] + [tab, tab],
        out_specs=[tok(w_) for w_ in widths] + [tok(mw)] * 3,
        out_shape=[jax.ShapeDtypeStruct((bsz, n, w_), dt) for w_, dt in zip(widths, dtypes)]
                  + [jax.ShapeDtypeStruct((bsz, n, mw), BF16)] * 3,
        compiler_params=_cparams(("parallel", "arbitrary")),
        name="inproj",
    )(x, mod, g, w, cos, sin, *res, cosm, sinm)


def _swa_kernel(sink_ref, q_ref, kvp_ref, kvc_ref, kvn_ref, kvx_ref, band_ref, o_ref):
    i = pl.program_id(1)
    nt = pl.num_programs(1)
    blk = SWA_BLOCK
    lane = lax.broadcasted_iota(jnp.int32, (1, LANES), 1)
    hsel = [lane < HEAD_DIM, lane >= HEAD_DIM]
    hmask = [m.astype(F32) for m in hsel]
    q = q_ref[0]
    cur = kvc_ref[0]
    kvx = kvx_ref[0]
    latent = i > 0
    blocks = [(kvp_ref[0], cur[:blk], cur[blk:]), (cur[:blk], cur[blk:], kvn_ref[0])]
    valid = [(latent & (i > 1), latent, latent), (latent, latent, latent & (i < nt - 1))]
    rblk = lax.broadcasted_iota(jnp.int32, (4 * blk, 1), 0) // blk
    sk = jnp.where(rblk == 0, sink_ref[0], jnp.where(rblk == 1, sink_ref[2],
                                                       jnp.where(rblk == 2, sink_ref[1], sink_ref[3]))) * LOG2E
    zero = jnp.zeros((1, LANES), F32)
    s, vcat = [], []
    for u in range(2):
        qu = q[u * blk:(u + 1) * blk]
        qs = jnp.concatenate([jnp.where(hsel[kh], qu[:, g * LANES:(g + 1) * LANES], 0)
                              for g in range(2) for kh in range(2)], axis=0)
        kvs = list(blocks[u]) + [kvx]
        kcat = jnp.concatenate([t[:, :LANES] for t in kvs], axis=0)
        vcat.append(jnp.concatenate([t[:, LANES:] for t in kvs], axis=0))
        pen = jnp.concatenate([zero + jnp.where(ok, 0.0, NEG_INF) for ok in valid[u]] + [zero, zero], axis=-1)
        s.append(_dot_nt(qs, kcat) + band_ref[...] + pen)
    p, l = [], []
    for u in range(2):
        m = jnp.maximum(jnp.max(s[u], axis=-1, keepdims=True), sk)
        pu = jnp.exp2(s[u] - m)
        l.append(jnp.sum(pu, axis=-1, keepdims=True) + jnp.exp2(sk - m))
        p.append(pu.astype(BF16))
    for u in range(2):
        o = _dot(p[u], vcat[u]) * (1.0 / l[u])
        outs = [o[(2 * g) * blk:(2 * g + 1) * blk] * hmask[0] + o[(2 * g + 1) * blk:(2 * g + 2) * blk] * hmask[1]
                for g in range(2)]
        o_ref[0, u * blk:(u + 1) * blk, :] = jnp.concatenate(outs, axis=-1).astype(o_ref.dtype)


def _swa(q, kv, sink, band):
    bsz, n, _ = q.shape
    blk = SWA_BLOCK
    tm = 2 * blk
    first = CTX_LEN // blk
    last = n // blk - 1
    return pl.pallas_call(
        _swa_kernel,
        grid=(bsz, n // tm),
        in_specs=[
            pl.BlockSpec(memory_space=pltpu.SMEM),
            pl.BlockSpec((1, tm, 256), lambda b, i: (b, i, 0)),
            pl.BlockSpec((1, blk, 256), lambda b, i: (b, jnp.clip(2 * i - 1, first, last), 0)),
            pl.BlockSpec((1, tm, 256), lambda b, i: (b, jnp.maximum(i, 1), 0)),
            pl.BlockSpec((1, blk, 256), lambda b, i: (b, jnp.clip(2 * i + 2, first, last), 0)),
            pl.BlockSpec((1, CTX_LEN, 256), lambda b, i: (b, 0, 0)),
            _resident(band.shape),
        ],
        out_specs=pl.BlockSpec((1, tm, 256), lambda b, i: (b, i, 0)),
        out_shape=jax.ShapeDtypeStruct((bsz, n, 256), BF16),
        compiler_params=_cparams(("parallel", "arbitrary")),
        name="swa",
    )(sink, q, kv, kv, kv, kv, band)


def _na_kernel(q_ref, kv_ref, tab_ref, o_ref, *, rows):
    i = pl.program_id(1)
    w = GRID_W
    nctx = CTX_LEN // w
    nloc = NA_KR * w
    lane = lax.broadcasted_iota(jnp.int32, (1, GROUP_W), 1) // HEAD_DIM
    hsel = [lane == h for h in range(N_HEADS)]
    hmask = [m.astype(F32) for m in hsel]
    kv_ctx = kv_ref[0, pl.ds(0, CTX_LEN), :]
    s, vcat = [], []
    for u in range(NA_UNITS):
        j = NA_UNITS * i + u
        r = jnp.maximum(j - nctx, 0)
        rs = jnp.clip(r - NA_KR // 2, 0, rows - NA_KR)
        tab_i = jnp.where(j < nctx, NA_KR, r - rs)
        qu = q_ref[0, u * w:(u + 1) * w, :]
        qs = jnp.concatenate([jnp.where(hsel[h], qu, 0) for h in range(N_HEADS)], axis=0)
        start = pl.multiple_of(CTX_LEN + rs * w, w)
        kv_loc = kv_ref[0, pl.ds(start, nloc), :]
        kcat = jnp.concatenate([kv_loc[:, :GROUP_W], kv_ctx[:, :GROUP_W]], axis=0)
        vcat.append(jnp.concatenate([kv_loc[:, GROUP_W:], kv_ctx[:, GROUP_W:]], axis=0))
        su = _dot_nt(qs, kcat)
        s.append(jnp.concatenate([su[:, :nloc] + tab_ref[tab_i], su[:, nloc:]], axis=-1))
    p, l = [], []
    for u in range(NA_UNITS):
        pu = jnp.exp2(s[u] - jnp.max(s[u], axis=-1, keepdims=True))
        l.append(jnp.sum(pu, axis=-1, keepdims=True))
        p.append(pu.astype(BF16))
    for u in range(NA_UNITS):
        o = _dot(p[u], vcat[u]) * (1.0 / l[u])
        acc = o[0:w] * hmask[0]
        for h in range(1, N_HEADS):
            acc = acc + o[h * w:(h + 1) * w] * hmask[h]
        o_ref[0, u * w:(u + 1) * w, :] = acc.astype(o_ref.dtype)


def _na(q, kv, tab):
    bsz, n, _ = q.shape
    w = GRID_W
    rows = (n - CTX_LEN) // w
    assert rows >= NA_KR
    return pl.pallas_call(
        functools.partial(_na_kernel, rows=rows),
        grid=(bsz, n // (NA_UNITS * w)),
        in_specs=[
            pl.BlockSpec((1, NA_UNITS * w, 256), lambda b, i: (b, i, 0)),
            pl.BlockSpec((1, n, 512), lambda b, i: (b, 0, 0)),
            _resident(tab.shape),
        ],
        out_specs=pl.BlockSpec((1, NA_UNITS * w, 256), lambda b, i: (b, i, 0)),
        out_shape=jax.ShapeDtypeStruct((bsz, n, 256), BF16),
        compiler_params=_cparams(("parallel", "arbitrary")),
        name="na",
    )(q, kv, tab)


def _mla_attn_kernel(q_ref, k_ref, v_ref, o_ref):
    i = pl.program_id(1)

    def attend(nkeys):
        head = lambda ref, h, rows: ref[0, pl.ds(0, rows), h * LANES:(h + 1) * LANES]
        s, p = {}, {}
        for t in range(N_HEADS + 2):
            if t < N_HEADS:
                s[t] = _dot_nt(head(q_ref, t, TOKEN_TILE), head(k_ref, t, nkeys))
            if 0 <= t - 1 < N_HEADS:
                sh = s.pop(t - 1)
                p[t - 1] = jnp.exp2(sh - jnp.max(sh, axis=-1, keepdims=True)).astype(BF16)
            if 0 <= t - 2 < N_HEADS:
                h = t - 2
                o = _dot(p.pop(h), head(v_ref, h, nkeys))
                o_ref[0, :, h * LANES:(h + 1) * LANES] = (o * (1.0 / o[:, MLA_V:MLA_V + 1])).astype(o_ref.dtype)

    @pl.when(i == 0)
    def _():
        attend(CTX_LEN)

    @pl.when(i > 0)
    def _():
        attend(k_ref.shape[1])


def _mla_attn(q, k, v):
    bsz, n, w = q.shape
    tq = TOKEN_TILE
    return pl.pallas_call(
        _mla_attn_kernel,
        grid=(bsz, n // tq),
        in_specs=[
            pl.BlockSpec((1, tq, w), lambda b, i: (b, i, 0)),
            pl.BlockSpec((1, n, w), lambda b, i: (b, 0, 0)),
            pl.BlockSpec((1, n, w), lambda b, i: (b, 0, 0)),
        ],
        out_specs=pl.BlockSpec((1, tq, w), lambda b, i: (b, i, 0)),
        out_shape=jax.ShapeDtypeStruct((bsz, n, w), BF16),
        compiler_params=_cparams(("parallel", "arbitrary")),
        name="mla_attn",
    )(q, k, v)


def _bd(m, bd_mask):
    return jnp.concatenate([m] * N_HEADS, axis=0) * bd_mask


def _dn_local_kernel(x_ref, xp_ref, xn_ref, ab_ref, cw_ref, alog_ref, dtb_ref, bdm_ref, exp_ref, stm_ref,
                     pk_ref, egl_ref):
    i = pl.program_id(1)
    nt = pl.num_programs(1)
    tm = TOKEN_TILE
    c = DN_CHUNK
    nch = tm // c
    nb = x_ref.shape[0]
    row = lax.broadcasted_iota(jnp.int32, (tm, 1), 0)
    lane = lax.broadcasted_iota(jnp.int32, (1, LANES), 1)
    bdm = bdm_ref[...]
    fronts = [_dn_front(x_ref[bi], xp_ref[bi][7:8], xn_ref[bi][0:1], ab_ref[bi], cw_ref[...], alog_ref[...],
                        dtb_ref[...], bdm, exp_ref[...], row, lane, i, nt) for bi in range(nb)]

    eye = stm_ref[4]
    tile_b = eye.astype(BF16)
    chains = [(bi, d, ci) for bi in range(nb) for ci in range(nch) for d in range(2)]
    rows = lambda ci: slice(ci * c, (ci + 1) * c)
    kq = {}
    for bi in range(nb):
        q, k, _, _, bexps = fronts[bi]
        for ci in range(nch):
            kc = k[rows(ci)]
            kt_bd = _dot_tn(kc.astype(BF16), tile_b).astype(BF16) * bdm
            lhs = jnp.concatenate([kc * bexps[0][rows(ci)], kc * bexps[1][rows(ci)], q[rows(ci)]], axis=0)
            kq[bi, ci] = _dot(lhs.astype(BF16), kt_bd)
    lbs, xinvs = {}, {}
    for bi, d, ci in chains:
        q, k, _, gcols, _ = fronts[bi]
        sl = rows(ci)
        gcol = gcols[d][sl]
        grow = jnp.sum(gcol * eye, axis=0, keepdims=True)
        incl, strict = stm_ref[2 * d], stm_ref[2 * d + 1]
        dec_incl = jnp.exp((gcol - grow) * incl) * incl
        lmat = kq[bi, ci][d * c:(d + 1) * c] * (dec_incl * strict)
        egc = jnp.exp(gcol)
        glast = gcol[c - 1:c] if d == 0 else gcol[0:1]
        pk_ref[d, bi, sl, PK_IN] = (kq[bi, ci][2 * c:3 * c] * dec_incl).astype(BF16)
        pk_ref[d, bi, sl, PK_QG] = (q[sl] * egc).astype(BF16)
        pk_ref[d, bi, sl, PK_KD] = (k[sl] * jnp.exp(glast - gcol)).astype(BF16)
        egl_ref[d, bi, ci * 8:(ci + 1) * 8, :] = jnp.broadcast_to(jnp.exp(glast), (8, GROUP_W))
        lbs[bi, d, ci] = lmat.astype(BF16)
        xinvs[bi, d, ci] = eye - lmat * stm_ref[DN_LEVEL0 + 6 * d]
    for lvl in range(1, 6):
        ys = {}
        for ch in chains:
            off = _bd(lbs[ch] * stm_ref[DN_LEVEL0 + 6 * ch[1] + lvl].astype(BF16), bdm)
            ys[ch] = _dot(xinvs[ch].astype(BF16), off)
        for ch in chains:
            xinvs[ch] = xinvs[ch] - _dot(ys[ch].astype(BF16), _bd(xinvs[ch].astype(BF16), bdm))
    for bi, d, ci in chains:
        _, k, v, gcols, bexps = fronts[bi]
        sl = rows(ci)
        xb = xinvs[bi, d, ci].astype(BF16)
        kb = k[sl] * bexps[d][sl]
        pk_ref[d, bi, sl, PK_U] = _dot(xb, _bd((v[sl] * bexps[d][sl]).astype(BF16), bdm)).astype(BF16)
        pk_ref[d, bi, sl, PK_W] = _dot(xb, _bd((kb * jnp.exp(gcols[d][sl])).astype(BF16), bdm)).astype(BF16)


def _dn_front(x, prev_row, next_row, ab, cw, alog, dtb, bdm, expand, row, lane, i, nt):
    tm = TOKEN_TILE
    c = DN_CHUNK
    prev_row = prev_row * (i > 1).astype(F32)
    next_row = next_row * ((i > 0) & (i < nt - 1)).astype(F32)
    xm1 = jnp.where(row == 0, prev_row, pltpu.roll(x, 1, 0))
    xp1 = jnp.where(row == tm - 1, next_row, pltpu.roll(x, tm - 1, 0))
    h = _silu(cw[0:1] * xm1 + cw[1:2] * x + cw[2:3] * xp1)
    q, k, v = h[:, 0:256], h[:, 256:512], h[:, 512:768]
    q = q * lax.rsqrt(_dot((q * q).astype(BF16), bdm) + EPS) * (HEAD_DIM ** -0.5)
    k = k * lax.rsqrt(_dot((k * k).astype(BF16), bdm) + EPS)

    z = ab + dtb
    softplus = jnp.maximum(z, 0.0) + jnp.log(1.0 + jnp.exp(-jnp.abs(z)))
    t = jnp.where(lane < 2 * N_HEADS, -jnp.exp(alog) * softplus, _sigmoid(ab))
    t = jnp.where(lane < 4 * N_HEADS, t, 0.0)
    rowc = row % c
    pre, suf = t, t
    step = 1
    while step < c:
        pre = pre + jnp.where(rowc >= step, pltpu.roll(pre, step, 0), 0.0)
        suf = suf + jnp.where(rowc < c - step, pltpu.roll(suf, tm - step, 0), 0.0)
        step *= 2
    t = jnp.where(lane < N_HEADS, pre, jnp.where(lane < 2 * N_HEADS, suf, t))
    hi = t.astype(BF16).astype(F32)
    rem = t - hi
    mid = rem.astype(BF16).astype(F32)
    comb = (hi + pltpu.roll(mid, 4 * N_HEADS, 1) + pltpu.roll(rem - mid, 8 * N_HEADS, 1)).astype(BF16)
    ex = _dot(comb, expand)
    return q, k, v, [ex[:, 0:256], ex[:, 256:512]], [ex[:, 512:768], ex[:, 768:1024]]


def _dn_local(qkv, ab, cw, alog, dtb, bdm, expand, stm):
    bsz, n, _ = qkv.shape
    tm = TOKEN_TILE
    nt = n // tm
    r8 = tm // 8
    nb = 2 if bsz % 2 == 0 else 1
    tok = lambda w_: pl.BlockSpec((nb, tm, w_), lambda b, i: (b, i, 0))
    outs = [jax.ShapeDtypeStruct((2, bsz, n, PK_WIDTH), BF16), jax.ShapeDtypeStruct((2, bsz, n // 8, 256), F32)]
    return pl.pallas_call(
        _dn_local_kernel,
        grid=(bsz // nb, nt),
        in_specs=[
            tok(768),
            pl.BlockSpec((nb, 8, 768), lambda b, i: (b, jnp.maximum(i * r8 - 1, 0), 0)),
            pl.BlockSpec((nb, 8, 768), lambda b, i: (b, jnp.minimum((i + 1) * r8, nt * r8 - 1), 0)),
            tok(128),
        ] + [_resident(a.shape) for a in (cw, alog, dtb, bdm, expand, stm)],
        out_specs=[pl.BlockSpec((2, nb, tm, PK_WIDTH), lambda b, i: (0, b, i, 0)),
                   pl.BlockSpec((2, nb, tm // 8, 256), lambda b, i: (0, b, i, 0))],
        out_shape=outs,
        compiler_params=_cparams(("parallel", "arbitrary")),
        name="dn_local",
    )(qkv, qkv, qkv, ab, cw, alog, dtb, bdm, expand, stm)


def _dn_scan_kernel(pf_ref, ef_ref, pb_ref, eb_ref, bdm_ref, of_ref, ob_ref, s_ref):
    c = DN_CHUNK
    nch = TOKEN_TILE // c
    nb = of_ref.shape[0]

    @pl.when(pl.program_id(1) == 0)
    def _():
        s_ref[...] = jnp.zeros_like(s_ref)

    bdm = bdm_ref[...]
    bdm_f = bdm.astype(F32)
    ins = ((pf_ref, ef_ref), (pb_ref, eb_ref))
    outs = (of_ref, ob_ref)
    for step in range(nch):
        chains = [(d, bi, step if d == 0 else nch - 1 - step) for bi in range(nb) for d in range(2)]
        rows = lambda ci: slice(ci * c, (ci + 1) * c)
        ws, vb = {}, {}
        for d, bi, ci in chains:
            p_ref = ins[d][0]
            lhs = jnp.concatenate([p_ref[0, bi, rows(ci), PK_W], p_ref[0, bi, rows(ci), PK_QG]], axis=0)
            ws[d, bi] = _dot(lhs, s_ref[d, bi].astype(BF16))
        for d, bi, ci in chains:
            p_ref = ins[d][0]
            vb[d, bi] = (p_ref[0, bi, rows(ci), PK_U].astype(F32) - ws[d, bi][:c]).astype(BF16)
            outs[d][bi, rows(ci), :] = ws[d, bi][c:] + _dot(p_ref[0, bi, rows(ci), PK_IN], _bd(vb[d, bi], bdm))
        for d, bi, ci in chains:
            p_ref, e_ref = ins[d]
            egl = e_ref[0, bi, ci * 8:ci * 8 + 1, :]
            s_ref[d, bi] = s_ref[d, bi] * egl + _dot_tn(p_ref[0, bi, rows(ci), PK_KD], vb[d, bi]) * bdm_f


def _dn_scan(pk, egl, bdm):
    _, bsz, n, _ = pk.shape
    tm = TOKEN_TILE
    nt = n // tm
    nb = 4 if bsz % 4 == 0 else (2 if bsz % 2 == 0 else 1)
    bidx = lambda s: jnp.where(s == 0, 0, nt - s)
    f4 = lambda r, w_: pl.BlockSpec((1, nb, r, w_), lambda b, s: (0, b, s, 0))
    b4 = lambda r, w_: pl.BlockSpec((1, nb, r, w_), lambda b, s: (1, b, bidx(s), 0))
    return pl.pallas_call(
        _dn_scan_kernel,
        grid=(bsz // nb, nt),
        in_specs=[f4(tm, PK_WIDTH), f4(tm // 8, 256), b4(tm, PK_WIDTH), b4(tm // 8, 256), _resident(bdm.shape)],
        out_specs=[pl.BlockSpec((nb, tm, 256), lambda b, s: (b, s, 0)),
                   pl.BlockSpec((nb, tm, 256), lambda b, s: (b, bidx(s), 0))],
        out_shape=[jax.ShapeDtypeStruct((bsz, n, 256), F32)] * 2,
        scratch_shapes=[pltpu.VMEM((2, nb, GROUP_W, GROUP_W), F32)],
        compiler_params=_cparams(("parallel", "arbitrary")),
        name="dn_scan",
    )(pk, egl, pk, egl, bdm)


def _half_swap(n):
    q = n // 4
    i = np.arange(n)
    return np.where((i % (2 * q)) < q, i + q, i - q)


def _inproj_columns():
    off = np.cumsum((0,) + IN_SIZES)
    z = IN_PROJ
    sw64, sw32 = _half_swap(64), _half_swap(32)
    gg, kk, dd = np.meshgrid(np.arange(2), np.arange(2), np.arange(64), indexing="ij")
    swa_q = (off[0] + kk * 128 + gg * 64 + dd).reshape(-1)
    swa_qs = (off[0] + kk * 128 + gg * 64 + sw64[dd]).reshape(-1)
    kk2, dd2 = np.meshgrid(np.arange(2), np.arange(64), indexing="ij")
    swa_k = (off[1] + kk2 * 64 + dd2).reshape(-1)
    swa_ks = (off[1] + kk2 * 64 + sw64[dd2]).reshape(-1)
    zeros = lambda n: np.full((n,), z)
    cols = np.concatenate([
        swa_q, swa_qs, swa_k, swa_ks, off[2] + np.arange(128),
        off[3] + np.arange(768), off[4] + np.arange(256),
        off[5] + np.arange(8), off[6] + np.arange(8), zeros(112),
        off[7] + np.arange(256), off[8] + np.arange(128),
        zeros(64), off[9] + np.arange(32), zeros(32),
        zeros(64), off[9] + sw32, zeros(32),
        off[10] + np.arange(256), off[11] + np.arange(256), off[12] + np.arange(256),
    ])
    assert cols.shape == (P_TOTAL,)
    return cols


def _gather_cols(w, cols):
    wz = jnp.concatenate([w, jnp.zeros(w.shape[:-1] + (1,), w.dtype)], axis=-1)
    return jnp.take(wz, jnp.asarray(cols), axis=-1)


def _mla_columns():
    dqk = MLA_NOPE + MLA_ROPE
    zq, zkv = N_HEADS * dqk, N_HEADS * (MLA_NOPE + MLA_V)
    sw32 = _half_swap(32)
    wq, wqs, wk, wv = [], [], [], []
    for h in range(N_HEADS):
        wq += [h * dqk + np.arange(dqk), np.full((32,), zq)]
        wqs += [np.full((64,), zq), h * dqk + MLA_NOPE + sw32, np.full((32,), zq)]
        wk += [h * 128 + np.arange(64), np.full((64,), zkv)]
        wv += [h * 128 + 64 + np.arange(64), np.full((64,), zkv)]
    return tuple(np.concatenate(c) for c in (wq, wqs, wk, wv))


def _wout_rows():
    z = 4 * GROUP_W
    gg, kk, dd = np.meshgrid(np.arange(2), np.arange(2), np.arange(64), indexing="ij")
    ya = (kk * 128 + gg * 64 + dd).reshape(-1)
    ym = np.concatenate([np.concatenate([2 * GROUP_W + h * 64 + np.arange(64), np.full((64,), z)])
                         for h in range(N_HEADS)])
    return np.concatenate([ya, GROUP_W + np.arange(256), ym, 3 * GROUP_W + np.arange(256)])


def _rope_tables(seq):
    t = np.arange(seq)
    row = (t // GRID_W).astype(np.float32)
    col = (t % GRID_W).astype(np.float32)

    def cs(n):
        inv = (np.float32(ROPE_THETA) ** (-np.arange(0, n, 2, dtype=np.float32) / np.float32(n))).astype(np.float32)
        ar, ac = row[:, None] * inv[None, :], col[:, None] * inv[None, :]
        cos = np.concatenate([np.cos(ar)] * 2 + [np.cos(ac)] * 2, axis=-1)
        sin = np.concatenate([-np.sin(ar), np.sin(ar), -np.sin(ac), np.sin(ac)], axis=-1)
        return cos.astype(np.float32), sin.astype(np.float32)

    cos64, sin64 = cs(HEAD_DIM // 2)
    cos32, sin32 = cs(MLA_ROPE // 2)
    ones = lambda w_: np.ones((CTX_LEN, w_), np.float32)
    zeros = lambda w_: np.zeros((CTX_LEN, w_), np.float32)
    cos_a = np.concatenate([ones(128), np.concatenate([cos64, cos64], axis=-1)], axis=0)
    sin_a = np.concatenate([zeros(128), np.concatenate([sin64, sin64], axis=-1)], axis=0)
    pad1, pad0 = np.ones((seq, 64), np.float32), np.zeros((seq, 32), np.float32)
    cos_m = np.concatenate([np.concatenate([ones(96), zeros(32)], axis=-1),
                            np.concatenate([pad1, cos32, pad0], axis=-1)], axis=0)
    sin_m = np.concatenate([zeros(128),
                            np.concatenate([np.zeros((seq, 64), np.float32), sin32, pad0], axis=-1)], axis=0)
    return tuple(jnp.asarray(a) for a in (cos_a, sin_a, cos_m, sin_m))


def _swa_band():
    i = np.arange(4 * SWA_BLOCK)[:, None] % SWA_BLOCK
    j = np.arange(3 * SWA_BLOCK + CTX_LEN)[None, :]
    ok = (j >= 3 * SWA_BLOCK) | ((j >= i) & (j <= i + 2 * SWA_WINDOW))
    return jnp.asarray(np.where(ok, 0.0, NEG_INF), F32)


def _na_bias_table(rpb):
    w = GRID_W
    qc = np.arange(w)[:, None]
    kc = np.arange(w)[None, :]
    cs = np.clip(qc - NA_KC // 2, 0, w - NA_KC)
    ok = (kc >= cs) & (kc < cs + NA_KC)
    dc = np.clip(kc - qc + NA_KC - 1, 0, 2 * NA_KC - 2)
    onehot = (np.arange(2 * NA_KC - 1)[None, :, None] == dc[:, None, :]).astype(np.float32)
    full = jnp.einsum("hrd,qdk->hrqk", rpb, jnp.asarray(onehot), precision=HIGHEST)
    full = jnp.where(jnp.asarray(ok)[None, None], full, NEG_INF)
    tabs = [jnp.transpose(full[:, NA_KR - 1 - dl:2 * NA_KR - 1 - dl], (0, 2, 1, 3)).reshape(N_HEADS * w, NA_KR * w)
            for dl in range(NA_KR)]
    tabs.append(jnp.full((N_HEADS * w, NA_KR * w), NEG_INF, F32))
    return jnp.stack(tabs)


def _dn_constants():
    c = DN_CHUNK
    hh = np.arange(GROUP_W) // HEAD_DIM
    bd = (hh[:, None] == hh[None, :])
    expand = np.zeros((LANES, 4 * GROUP_W), np.float32)
    for piece in range(3):
        for d in range(2):
            expand[16 * piece + d * 4 + hh, d * GROUP_W + np.arange(GROUP_W)] = 1.0
            expand[16 * piece + 8 + d * 4 + hh, (2 + d) * GROUP_W + np.arange(GROUP_W)] = 1.0
    i = np.arange(c)[:, None]
    jj = (np.arange(GROUP_W) % c)[None, :]
    masks = [(jj <= i), (jj < i), (jj >= i), (jj > i), (jj == i)]
    for d in range(2):
        for lvl in range(6):
            b = 1 << lvl
            same = (i // (2 * b)) == (jj // (2 * b))
            lo, hi = (jj % (2 * b)) < b, (i % (2 * b)) >= b
            masks.append(same & (lo & hi if d == 0 else ~lo & ~hi))
    stm = np.stack(masks).astype(np.float32)
    return jnp.asarray(bd, BF16), jnp.asarray(expand, BF16), jnp.asarray(stm)


def kernel(x, c, ctx, c_ctx, ada_w, ada_b, norm1_g, ffn1_wg, ffn1_wu, ffn1_wd, norm2_g, w_in, swa_sink,
           dn_conv_w, dn_a_log, dn_dt_bias, dn_norm_g, mla_q_norm_g, mla_w_uq, mla_kv_norm_g, mla_w_ukv,
           na_rpb, w_out, norm3_g, ffn2_wg, ffn2_wu, ffn2_wd, final_norm_g):
    bsz, seq, d = x.shape
    depth = ada_w.shape[0]
    assert d == D_MODEL and ctx.shape[1] == CTX_LEN and seq % TOKEN_TILE == 0 and bsz <= CTX_ROW
    assert CTX_ROW % BATCH_ROWS == 0 and CTX_ROW + BATCH_ROWS <= MOD_ROWS

    cvec = jnp.zeros((MOD_ROWS, d), F32).at[:bsz].set(c).at[CTX_ROW:].set(c_ctx)
    mod = _modulation(cvec, ada_w, ada_b).reshape(depth, MOD_ROWS, 3, 3, d)

    cos_a, sin_a, cos_m, sin_m = _rope_tables(seq)
    band = _swa_band()
    bdm, expand, stm = _dn_constants()
    in_cols = _inproj_columns()
    mq, mqs, mk, mv = _mla_columns()
    wo_rows = _wout_rows()
    row2 = lambda v: v.reshape(1, -1)

    h = x
    for i in range(depth):
        last = i == depth - 1
        bf = lambda a: a.astype(BF16)
        h = _ffn(h, mod[i, :, 0], row2(norm1_g[i]), bf(ffn1_wg[i]), bf(ffn1_wu[i]), bf(ffn1_wd[i]),
                 ctx=ctx if i == 0 else None)
        wp = bf(_gather_cols(w_in[i], in_cols))
        mla = (row2(mla_q_norm_g[i]), row2(mla_kv_norm_g[i]),
               bf(_gather_cols(mla_w_uq[i], mq)), bf(_gather_cols(mla_w_uq[i], mqs)),
               bf(_gather_cols(mla_w_ukv[i], mk)), bf(_gather_cols(mla_w_ukv[i], mv)), cos_m, sin_m)
        swa_q, swa_kv, dn_qkv, dn_z, dn_ab, na_q, na_kv, mla_q, mla_k, mla_v = _inproj(
            h, mod[i, :, 1], row2(norm2_g[i]), wp, cos_a, sin_a, mla)

        ya = _swa(swa_q, swa_kv, swa_sink[i], band)
        yn = _na(na_q, na_kv, _na_bias_table(na_rpb[i] * LOG2E))
        ym = _mla_attn(mla_q, mla_k, mla_v)
        pad = lambda v: jnp.zeros((1, LANES), F32).at[0, :2 * N_HEADS].set(v.reshape(-1))
        pk, egl = _dn_local(dn_qkv, dn_ab, dn_conv_w[i], pad(dn_a_log[i]), pad(dn_dt_bias[i]), bdm, expand, stm)
        o_f, o_b = _dn_scan(pk, egl, bdm)

        wo = bf(jnp.take(jnp.concatenate([w_out[i], jnp.zeros((1, d), F32)], axis=0), jnp.asarray(wo_rows), axis=0))
        mix = (ya, o_f, o_b, dn_z, ym, yn, mod[i, :, 1], row2(jnp.tile(dn_norm_g[i], N_HEADS)), bdm, wo)
        h = _ffn(h, mod[i, :, 2], row2(norm3_g[i]), bf(ffn2_wg[i]), bf(ffn2_wu[i]), bf(ffn2_wd[i]), mix=mix,
                 final_g=row2(final_norm_g) if last else None, latent_only=last)
    return h
```

```python
import functools

import numpy as np
import jax
import jax.numpy as jnp
from jax import lax
from jax.experimental import pallas as pl
from jax.experimental.pallas import tpu as pltpu

F32 = jnp.float32
BF16 = jnp.bfloat16
HIGHEST = lax.Precision.HIGHEST

D_MODEL = 1024
CTX_LEN = 256
GRID_W = 64
HEAD_DIM = 64
N_HEADS = 4
GROUP_W = N_HEADS * HEAD_DIM
D_FF = 2816
N_MOD = 9
ROPE_THETA = 10000.0
EPS = 1e-6
NEG_INF = -1e30
LOG2E = 1.4426950408889634
SWA_WINDOW = 128
SWA_BLOCK = 128
DN_CHUNK = 64
DN_LEVEL0 = 5
PK_U, PK_W, PK_QG, PK_KD, PK_IN = (slice(j * 256, (j + 1) * 256) for j in range(5))
PK_WIDTH = 5 * 256
MLA_Q_RANK = 256
MLA_KV_RANK = 128
MLA_NOPE = 64
MLA_ROPE = 32
MLA_V = 64
NA_KR = 8
NA_KC = 16
NA_UNITS = 4
IN_SIZES = (256, 128, 128, 768, 256, 8, 8, 256, 128, 32, 256, 256, 256)
IN_PROJ = sum(IN_SIZES)

LANES = 128
TOKEN_TILE = 256
VMEM_LIMIT = 56 * 1024 * 1024
MOD_ROWS = 16
CTX_ROW = 8
BATCH_ROWS = 2

P_SWA_Q, P_SWA_QS, P_SWA_K, P_SWA_KS, P_SWA_V = 0, 256, 512, 640, 768
P_DN_QKV, P_DN_Z, P_DN_AB = 896, 1664, 1920
P_MLA = 2048
P_NA_Q, P_NA_KV = 2688, 2944
P_TOTAL = 3456


def _cparams(sem):
    return pltpu.CompilerParams(dimension_semantics=sem, vmem_limit_bytes=VMEM_LIMIT)


def _dot(a, b, precision=None):
    return jnp.dot(a, b, preferred_element_type=F32, precision=precision)


def _dot_nt(a, b):
    return lax.dot_general(a, b, (((1,), (1,)), ((), ())), preferred_element_type=F32)


def _dot_tn(a, b):
    return lax.dot_general(a, b, (((0,), (0,)), ((), ())), preferred_element_type=F32)


def _sigmoid(x):
    return 1.0 / (1.0 + jnp.exp(-x))


def _silu(x):
    return x * _sigmoid(x)


def _resident(shape):
    nd = len(shape)
    return pl.BlockSpec(shape, lambda *_: (0,) * nd, pipeline_mode=pl.Buffered(1))


def _mod_kernel(c_ref, w_ref, b_ref, o_ref):
    s = _silu(c_ref[...])
    o_ref[0] = _dot(s, w_ref[0], HIGHEST) + b_ref[0]


def _modulation(cvec, ada_w, ada_b):
    n_layers = ada_w.shape[0]
    d = D_MODEL
    return pl.pallas_call(
        _mod_kernel,
        grid=(n_layers, N_MOD),
        in_specs=[
            pl.BlockSpec((MOD_ROWS, d), lambda l, j: (0, 0)),
            pl.BlockSpec((1, d, d), lambda l, j: (l, 0, j)),
            pl.BlockSpec((1, 1, d), lambda l, j: (l, 0, j)),
        ],
        out_specs=pl.BlockSpec((1, MOD_ROWS, d), lambda l, j: (l, 0, j)),
        out_shape=jax.ShapeDtypeStruct((n_layers, MOD_ROWS, N_MOD * d), F32),
        compiler_params=_cparams(("arbitrary", "arbitrary")),
        name="modulation",
    )(cvec, ada_w, ada_b.reshape(n_layers, 1, N_MOD * d))


def _mod_index(b, i, nb=1):
    return (jnp.where(i == 0, CTX_ROW // nb, b), 0, 0)


def _prenorm(x, g, shift, scale):
    ms = jnp.mean(x * x, axis=-1, keepdims=True)
    return (x * lax.rsqrt(ms + EPS) * g) * (1.0 + scale) + shift


def _group_mean_sq(x, ones_bd):
    return _dot((x * x).astype(BF16), ones_bd) * (1.0 / HEAD_DIM)


def _ffn_kernel(*refs, mix, final, split):
    it = iter(refs)
    x_ref = next(it)
    if split:
        ctx_ref = next(it)
    if mix:
        ya_ref, of_ref, ob_ref, z_ref, ym_ref, yn_ref = (next(it) for _ in range(6))
        ymc_ref = next(it) if mix == "ctx" else None
        mmod_ref, dng_ref, ones_ref, wo_ref = (next(it) for _ in range(4))
    mod_ref, g_ref, wg_ref, wu_ref, wd_ref = (next(it) for _ in range(5))
    if final:
        fg_ref = next(it)
    o_ref = next(it)

    nb, tm, _ = x_ref.shape
    xs, hs = [], []
    for bi in range(nb):
        x = x_ref[bi]
        if split:
            x = jnp.where(pl.program_id(1) == 0, ctx_ref[bi], x)
        if mix:
            o = of_ref[bi] + ob_ref[bi]
            yd = (o * lax.rsqrt(_group_mean_sq(o, ones_ref[...]) + EPS) * dng_ref[...]) * _silu(z_ref[bi])
            ym = ym_ref[bi] if ymc_ref is None else jnp.where(pl.program_id(1) == 0, ymc_ref[bi], ym_ref[bi])
            ycat = jnp.concatenate([ya_ref[bi], yd.astype(BF16), ym, yn_ref[bi]], axis=-1)
            x = x + mmod_ref[bi][2:3] * _dot(ycat, wo_ref[...])
        mod = mod_ref[bi]
        xs.append(x)
        hs.append(_prenorm(x, g_ref[...], mod[0:1], mod[1:2]).astype(BF16))
    h = jnp.concatenate(hs, axis=0)
    a = (_silu(_dot(h, wg_ref[...])) * _dot(h, wu_ref[...])).astype(BF16)
    y = _dot(a, wd_ref[...])
    for bi in range(nb):
        x = xs[bi] + (0.5 * mod_ref[bi][2:3]) * y[bi * tm:(bi + 1) * tm]
        if final:
            ms = jnp.mean(x * x, axis=-1, keepdims=True)
            x = x * lax.rsqrt(ms + EPS) * fg_ref[...]
        o_ref[bi] = x


def _ffn(x, mod, g, wg, wu, wd, mix=None, final_g=None, latent_only=False, ctx=None):
    bsz, n, d = x.shape
    tm = TOKEN_TILE
    nb = BATCH_ROWS if bsz % BATCH_ROWS == 0 else 1
    off = 1 if latent_only else 0
    tiles = n // tm - off + (ctx is not None)
    tok = lambda w: pl.BlockSpec((nb, tm, w), lambda b, i: (b, i + off, 0))
    modspec = pl.BlockSpec((nb, 3, d), lambda b, i: _mod_index(b, i + off, nb))
    if ctx is not None:
        assert mix is None and not latent_only and ctx.shape[1] == tm
        args = [x, ctx]
        specs = [pl.BlockSpec((nb, tm, d), lambda b, i: (b, jnp.maximum(i - 1, 0), 0)),
                 pl.BlockSpec((nb, tm, d), lambda b, i: (b, 0, 0))]
    else:
        args, specs = [x], [tok(d)]
    if mix is not None:
        ya, of, ob, z, (ym, ymc), yn, mmod, dng, ones_bd, wo = mix
        assert (ymc is None) == latent_only
        lat = pl.BlockSpec((nb, tm, ym.shape[-1]), lambda b, i: (b, jnp.maximum(i + off - 1, 0), 0))
        args += [ya, of, ob, z, ym, yn]
        specs += [tok(ya.shape[-1]), tok(of.shape[-1]), tok(ob.shape[-1]), tok(z.shape[-1]), lat, tok(yn.shape[-1])]
        if ymc is not None:
            args.append(ymc)
            specs.append(pl.BlockSpec((nb, tm, ymc.shape[-1]), lambda b, i: (b, 0, 0)))
        args += [mmod, dng, ones_bd, wo]
        specs += [modspec, _resident(dng.shape), _resident(ones_bd.shape), _resident(wo.shape)]
    args += [mod, g, wg, wu, wd]
    specs += [modspec, _resident(g.shape), _resident(wg.shape), _resident(wu.shape), _resident(wd.shape)]
    if final_g is not None:
        args.append(final_g)
        specs.append(_resident(final_g.shape))
    return pl.pallas_call(
        functools.partial(_ffn_kernel, mix=(None if mix is None else ("latent" if latent_only else "ctx")),
                          final=final_g is not None, split=ctx is not None),
        grid=(bsz // nb, tiles),
        in_specs=specs,
        out_specs=pl.BlockSpec((nb, tm, d), lambda b, i: (b, i, 0)),
        out_shape=jax.ShapeDtypeStruct((bsz, tiles * tm, d), F32),
        compiler_params=_cparams(("parallel", "arbitrary")),
        name="ffn_mix" if mix is not None else "ffn",
    )(*args)


def _inproj_kernel(x_ref, mod_ref, g_ref, w_ref, cos_ref, sin_ref,
                   gq_ref, gkv_ref, wq_ref, wqs_ref, wk_ref, wv_ref, vone_ref, cosm_ref, sinm_ref,
                   swaq_ref, swakv_ref, dnqkv_ref, dnz_ref, dnab_ref, naq_ref, nakv_ref,
                   mq_ref, mk_ref, mv_ref):
    nb, tm, _ = x_ref.shape
    g = g_ref[...]
    h = jnp.concatenate([_prenorm(x_ref[bi], g, mod_ref[bi][0:1], mod_ref[bi][1:2]).astype(BF16)
                         for bi in range(nb)], axis=0)
    u_all = _dot(h, w_ref[...])
    cos, sin = cos_ref[...], sin_ref[...]
    cos2 = jnp.concatenate([cos, cos], axis=-1)
    sin2 = jnp.concatenate([sin, sin], axis=-1)
    scale = HEAD_DIM ** -0.5 * LOG2E
    cosm, sinm = cosm_ref[...], sinm_ref[...]
    cos4 = jnp.concatenate([cosm] * N_HEADS, axis=-1)
    sin4 = jnp.concatenate([sinm] * N_HEADS, axis=-1)
    mscale = (MLA_NOPE + MLA_ROPE) ** -0.5 * LOG2E
    for bi in range(nb):
        u = u_all[bi * tm:(bi + 1) * tm]
        swaq_ref[bi] = ((u[:, P_SWA_Q:P_SWA_Q + 256] * cos2 + u[:, P_SWA_QS:P_SWA_QS + 256] * sin2) * scale).astype(BF16)
        k = u[:, P_SWA_K:P_SWA_K + 128] * cos + u[:, P_SWA_KS:P_SWA_KS + 128] * sin
        swakv_ref[bi] = jnp.concatenate([k, u[:, P_SWA_V:P_SWA_V + 128]], axis=-1).astype(BF16)
        dnqkv_ref[bi] = u[:, P_DN_QKV:P_DN_QKV + 768]
        dnz_ref[bi] = u[:, P_DN_Z:P_DN_Z + 256]
        dnab_ref[bi] = u[:, P_DN_AB:P_DN_AB + 128]
        naq_ref[bi] = (u[:, P_NA_Q:P_NA_Q + 256] * scale).astype(BF16)
        nakv_ref[bi] = u[:, P_NA_KV:P_NA_KV + 512].astype(BF16)

        cq, ckv = u[:, P_MLA:P_MLA + 256], u[:, P_MLA + 256:P_MLA + 384]
        kra, krb = u[:, P_MLA + 384:P_MLA + 512], u[:, P_MLA + 512:P_MLA + 640]
        nq = (cq * lax.rsqrt(jnp.mean(cq * cq, axis=-1, keepdims=True) + EPS) * gq_ref[...]).astype(BF16)
        nkv = (ckv * lax.rsqrt(jnp.mean(ckv * ckv, axis=-1, keepdims=True) + EPS) * gkv_ref[...]).astype(BF16)
        mq_ref[bi] = ((_dot(nq, wq_ref[...]) * cos4 + _dot(nq, wqs_ref[...]) * sin4) * mscale).astype(BF16)
        kr = kra * cosm + krb * sinm
        mk_ref[bi] = (_dot(nkv, wk_ref[...]) + jnp.concatenate([kr] * N_HEADS, axis=-1)).astype(BF16)
        mv_ref[bi] = (_dot(nkv, wv_ref[...]) + vone_ref[...]).astype(BF16)


def _inproj(x, mod, g, w, cos, sin, mla):
    bsz, n, d = x.shape
    tm = TOKEN_TILE
    gq, gkv, wq, wqs, wk, wv, cosm, sinm = mla
    vone = jnp.asarray((np.arange(N_HEADS * LANES) % LANES == MLA_V).astype(np.float32)).reshape(1, -1)
    widths = (256, 256, 768, 256, 128, 256, 512)
    dtypes = (BF16, BF16, F32, F32, F32, BF16, BF16)
    nb = BATCH_ROWS if bsz % BATCH_ROWS == 0 else 1
    tok = lambda w_: pl.BlockSpec((nb, tm, w_), lambda b, i: (b, i, 0))
    tab = pl.BlockSpec((tm, LANES), lambda b, i: (i, 0))
    res = [gq, gkv, wq, wqs, wk, wv, vone]
    mw = N_HEADS * LANES
    return pl.pallas_call(
        _inproj_kernel,
        grid=(bsz // nb, n // tm),
        in_specs=[tok(d), pl.BlockSpec((nb, 3, d), lambda b, i: _mod_index(b, i, nb)), _resident(g.shape),
                  _resident(w.shape), tab, tab]
                 + [_resident(a.shape) for a in res] + [tab, tab],
        out_specs=[tok(w_) for w_ in widths] + [tok(mw)] * 3,
        out_shape=[jax.ShapeDtypeStruct((bsz, n, w_), dt) for w_, dt in zip(widths, dtypes)]
                  + [jax.ShapeDtypeStruct((bsz, n, mw), BF16)] * 3,
        compiler_params=_cparams(("parallel", "arbitrary")),
        name="inproj",
    )(x, mod, g, w, cos, sin, *res, cosm, sinm)


def _swa_kernel(sink_ref, q_ref, kvp_ref, kvc_ref, kvn_ref, kvx_ref, band_ref, o_ref):
    i = pl.program_id(1)
    nt = pl.num_programs(1)
    blk = SWA_BLOCK
    lane = lax.broadcasted_iota(jnp.int32, (1, LANES), 1)
    hsel = [lane < HEAD_DIM, lane >= HEAD_DIM]
    hmask = [m.astype(F32) for m in hsel]
    q = q_ref[0]
    cur = kvc_ref[0]
    kvx = kvx_ref[0]
    latent = i > 0
    blocks = [(kvp_ref[0], cur[:blk], cur[blk:]), (cur[:blk], cur[blk:], kvn_ref[0])]
    valid = [(latent & (i > 1), latent, latent), (latent, latent, latent & (i < nt - 1))]
    rblk = lax.broadcasted_iota(jnp.int32, (4 * blk, 1), 0) // blk
    sk = jnp.where(rblk == 0, sink_ref[0], jnp.where(rblk == 1, sink_ref[2],
                                                       jnp.where(rblk == 2, sink_ref[1], sink_ref[3]))) * LOG2E
    zero = jnp.zeros((1, LANES), F32)
    s, vcat = [], []
    for u in range(2):
        qu = q[u * blk:(u + 1) * blk]
        qs = jnp.concatenate([jnp.where(hsel[kh], qu[:, g * LANES:(g + 1) * LANES], 0)
                              for g in range(2) for kh in range(2)], axis=0)
        kvs = list(blocks[u]) + [kvx]
        kcat = jnp.concatenate([t[:, :LANES] for t in kvs], axis=0)
        vcat.append(jnp.concatenate([t[:, LANES:] for t in kvs], axis=0))
        pen = jnp.concatenate([zero + jnp.where(ok, 0.0, NEG_INF) for ok in valid[u]] + [zero, zero], axis=-1)
        s.append(_dot_nt(qs, kcat) + band_ref[...] + pen)
    p, l = [], []
    for u in range(2):
        m = jnp.maximum(jnp.max(s[u], axis=-1, keepdims=True), sk)
        pu = jnp.exp2(s[u] - m)
        l.append(jnp.sum(pu, axis=-1, keepdims=True) + jnp.exp2(sk - m))
        p.append(pu.astype(BF16))
    for u in range(2):
        o = _dot(p[u], vcat[u]) * (1.0 / l[u])
        outs = [o[(2 * g) * blk:(2 * g + 1) * blk] * hmask[0] + o[(2 * g + 1) * blk:(2 * g + 2) * blk] * hmask[1]
                for g in range(2)]
        o_ref[0, u * blk:(u + 1) * blk, :] = jnp.concatenate(outs, axis=-1).astype(o_ref.dtype)


def _swa(q, kv, sink, band):
    bsz, n, _ = q.shape
    blk = SWA_BLOCK
    tm = 2 * blk
    first = CTX_LEN // blk
    last = n // blk - 1
    return pl.pallas_call(
        _swa_kernel,
        grid=(bsz, n // tm),
        in_specs=[
            pl.BlockSpec(memory_space=pltpu.SMEM),
            pl.BlockSpec((1, tm, 256), lambda b, i: (b, i, 0)),
            pl.BlockSpec((1, blk, 256), lambda b, i: (b, jnp.clip(2 * i - 1, first, last), 0)),
            pl.BlockSpec((1, tm, 256), lambda b, i: (b, jnp.maximum(i, 1), 0)),
            pl.BlockSpec((1, blk, 256), lambda b, i: (b, jnp.clip(2 * i + 2, first, last), 0)),
            pl.BlockSpec((1, CTX_LEN, 256), lambda b, i: (b, 0, 0)),
            _resident(band.shape),
        ],
        out_specs=pl.BlockSpec((1, tm, 256), lambda b, i: (b, i, 0)),
        out_shape=jax.ShapeDtypeStruct((bsz, n, 256), BF16),
        compiler_params=_cparams(("parallel", "arbitrary")),
        name="swa",
    )(sink, q, kv, kv, kv, kv, band)


def _na_kernel(q_ref, kv_ref, tab_ref, o_ref, *, rows):
    i = pl.program_id(1)
    w = GRID_W
    nctx = CTX_LEN // w
    nloc = NA_KR * w
    lane = lax.broadcasted_iota(jnp.int32, (1, GROUP_W), 1) // HEAD_DIM
    hsel = [lane == h for h in range(N_HEADS)]
    hmask = [m.astype(F32) for m in hsel]
    kv_ctx = kv_ref[0, pl.ds(0, CTX_LEN), :]
    s, vcat = [], []
    for u in range(NA_UNITS):
        j = NA_UNITS * i + u
        r = jnp.maximum(j - nctx, 0)
        rs = jnp.clip(r - NA_KR // 2, 0, rows - NA_KR)
        tab_i = jnp.where(j < nctx, NA_KR, r - rs)
        qu = q_ref[0, u * w:(u + 1) * w, :]
        qs = jnp.concatenate([jnp.where(hsel[h], qu, 0) for h in range(N_HEADS)], axis=0)
        start = pl.multiple_of(CTX_LEN + rs * w, w)
        kv_loc = kv_ref[0, pl.ds(start, nloc), :]
        kcat = jnp.concatenate([kv_loc[:, :GROUP_W], kv_ctx[:, :GROUP_W]], axis=0)
        vcat.append(jnp.concatenate([kv_loc[:, GROUP_W:], kv_ctx[:, GROUP_W:]], axis=0))
        su = _dot_nt(qs, kcat)
        s.append(jnp.concatenate([su[:, :nloc] + tab_ref[tab_i], su[:, nloc:]], axis=-1))
    p, l = [], []
    for u in range(NA_UNITS):
        pu = jnp.exp2(s[u] - jnp.max(s[u], axis=-1, keepdims=True))
        l.append(jnp.sum(pu, axis=-1, keepdims=True))
        p.append(pu.astype(BF16))
    for u in range(NA_UNITS):
        o = _dot(p[u], vcat[u]) * (1.0 / l[u])
        acc = o[0:w] * hmask[0]
        for h in range(1, N_HEADS):
            acc = acc + o[h * w:(h + 1) * w] * hmask[h]
        o_ref[0, u * w:(u + 1) * w, :] = acc.astype(o_ref.dtype)


def _na(q, kv, tab):
    bsz, n, _ = q.shape
    w = GRID_W
    rows = (n - CTX_LEN) // w
    assert rows >= NA_KR
    return pl.pallas_call(
        functools.partial(_na_kernel, rows=rows),
        grid=(bsz, n // (NA_UNITS * w)),
        in_specs=[
            pl.BlockSpec((1, NA_UNITS * w, 256), lambda b, i: (b, i, 0)),
            pl.BlockSpec((1, n, 512), lambda b, i: (b, 0, 0)),
            _resident(tab.shape),
        ],
        out_specs=pl.BlockSpec((1, NA_UNITS * w, 256), lambda b, i: (b, i, 0)),
        out_shape=jax.ShapeDtypeStruct((bsz, n, 256), BF16),
        compiler_params=_cparams(("parallel", "arbitrary")),
        name="na",
    )(q, kv, tab)


def _mla_attn_kernel(*refs, nkeys):
    *q_refs, k_ref, v_ref, o_ref = refs
    tq = TOKEN_TILE
    units = [(r, h) for r in range(len(q_refs)) for h in range(N_HEADS)]
    hl = lambda h: slice(h * LANES, (h + 1) * LANES)
    s, p = {}, {}
    for t in range(len(units) + 2):
        if t < len(units):
            r, h = units[t]
            s[t] = _dot_nt(q_refs[r][0, :, hl(h)], k_ref[0, pl.ds(0, nkeys), hl(h)])
        if 0 <= t - 1 < len(units):
            sh = s.pop(t - 1)
            p[t - 1] = jnp.exp2(sh - jnp.max(sh, axis=-1, keepdims=True)).astype(BF16)
        if 0 <= t - 2 < len(units):
            r, h = units[t - 2]
            o = _dot(p.pop(t - 2), v_ref[0, pl.ds(0, nkeys), hl(h)])
            o_ref[0, r * tq:(r + 1) * tq, hl(h)] = (o * (1.0 / o[:, MLA_V:MLA_V + 1])).astype(o_ref.dtype)


def _mla_attn(q, k, v, with_ctx):
    bsz, n, w = q.shape
    tq = TOKEN_TILE
    pairs = (n // tq - 1) // 2
    assert 2 * pairs + 1 == n // tq
    whole = pl.BlockSpec((1, n, w), lambda b, i: (b, 0, 0))
    lat = pl.pallas_call(
        functools.partial(_mla_attn_kernel, nkeys=n),
        grid=(bsz, pairs),
        in_specs=[pl.BlockSpec((1, tq, w), lambda b, i: (b, 2 * i + 1, 0)),
                  pl.BlockSpec((1, tq, w), lambda b, i: (b, 2 * i + 2, 0)), whole, whole],
        out_specs=pl.BlockSpec((1, 2 * tq, w), lambda b, i: (b, i, 0)),
        out_shape=jax.ShapeDtypeStruct((bsz, n - tq, w), BF16),
        compiler_params=_cparams(("parallel", "arbitrary")),
        name="mla_attn",
    )(q, q, k, v)
    if not with_ctx:
        return lat, None
    first = pl.BlockSpec((1, tq, w), lambda b: (b, 0, 0))
    ctx = pl.pallas_call(
        functools.partial(_mla_attn_kernel, nkeys=CTX_LEN),
        grid=(bsz,),
        in_specs=[first, first, first],
        out_specs=first,
        out_shape=jax.ShapeDtypeStruct((bsz, tq, w), BF16),
        compiler_params=_cparams(("parallel",)),
        name="mla_attn_ctx",
    )(q, k, v)
    return lat, ctx


def _bd(m, bd_mask):
    return jnp.concatenate([m] * N_HEADS, axis=0) * bd_mask


def _dn_local_kernel(x_ref, xp_ref, xn_ref, ab_ref, cw_ref, alog_ref, dtb_ref, bdm_ref, exp_ref, stm_ref,
                     pk_ref, egl_ref):
    i = pl.program_id(1)
    nt = pl.num_programs(1)
    tm = TOKEN_TILE
    c = DN_CHUNK
    nch = tm // c
    nb = x_ref.shape[0]
    row = lax.broadcasted_iota(jnp.int32, (tm, 1), 0)
    lane = lax.broadcasted_iota(jnp.int32, (1, LANES), 1)
    bdm = bdm_ref[...]
    fronts = [_dn_front(x_ref[bi], xp_ref[bi][7:8], xn_ref[bi][0:1], ab_ref[bi], cw_ref[...], alog_ref[...],
                        dtb_ref[...], bdm, exp_ref[...], row, lane, i, nt) for bi in range(nb)]

    eye = stm_ref[4]
    tile_b = eye.astype(BF16)
    chains = [(bi, d, ci) for bi in range(nb) for ci in range(nch) for d in range(2)]
    rows = lambda ci: slice(ci * c, (ci + 1) * c)
    kq = {}
    for bi in range(nb):
        q, k, _, _, bexps = fronts[bi]
        for ci in range(nch):
            kc = k[rows(ci)]
            kt_bd = _dot_tn(kc.astype(BF16), tile_b).astype(BF16) * bdm
            lhs = jnp.concatenate([kc * bexps[0][rows(ci)], kc * bexps[1][rows(ci)], q[rows(ci)]], axis=0)
            kq[bi, ci] = _dot(lhs.astype(BF16), kt_bd)
    lbs, xinvs = {}, {}
    for bi, d, ci in chains:
        q, k, _, gcols, _ = fronts[bi]
        sl = rows(ci)
        gcol = gcols[d][sl]
        grow = jnp.sum(gcol * eye, axis=0, keepdims=True)
        incl, strict = stm_ref[2 * d], stm_ref[2 * d + 1]
        dec_incl = jnp.exp((gcol - grow) * incl) * incl
        lmat = kq[bi, ci][d * c:(d + 1) * c] * (dec_incl * strict)
        egc = jnp.exp(gcol)
        glast = gcol[c - 1:c] if d == 0 else gcol[0:1]
        pk_ref[d, bi, sl, PK_IN] = (kq[bi, ci][2 * c:3 * c] * dec_incl).astype(BF16)
        pk_ref[d, bi, sl, PK_QG] = (q[sl] * egc).astype(BF16)
        pk_ref[d, bi, sl, PK_KD] = (k[sl] * jnp.exp(glast - gcol)).astype(BF16)
        egl_ref[d, bi, ci * 8:(ci + 1) * 8, :] = jnp.broadcast_to(jnp.exp(glast), (8, GROUP_W))
        lbs[bi, d, ci] = lmat.astype(BF16)
        xinvs[bi, d, ci] = eye - lmat * stm_ref[DN_LEVEL0 + 6 * d]
    for lvl in range(1, 6):
        ys = {}
        for ch in chains:
            off = _bd(lbs[ch] * stm_ref[DN_LEVEL0 + 6 * ch[1] + lvl].astype(BF16), bdm)
            ys[ch] = _dot(xinvs[ch].astype(BF16), off)
        for ch in chains:
            xinvs[ch] = xinvs[ch] - _dot(ys[ch].astype(BF16), _bd(xinvs[ch].astype(BF16), bdm))
    for bi, d, ci in chains:
        _, k, v, gcols, bexps = fronts[bi]
        sl = rows(ci)
        xb = xinvs[bi, d, ci].astype(BF16)
        kb = k[sl] * bexps[d][sl]
        pk_ref[d, bi, sl, PK_U] = _dot(xb, _bd((v[sl] * bexps[d][sl]).astype(BF16), bdm)).astype(BF16)
        pk_ref[d, bi, sl, PK_W] = _dot(xb, _bd((kb * jnp.exp(gcols[d][sl])).astype(BF16), bdm)).astype(BF16)


def _dn_front(x, prev_row, next_row, ab, cw, alog, dtb, bdm, expand, row, lane, i, nt):
    tm = TOKEN_TILE
    c = DN_CHUNK
    prev_row = prev_row * (i > 1).astype(F32)
    next_row = next_row * ((i > 0) & (i < nt - 1)).astype(F32)
    xm1 = jnp.where(row == 0, prev_row, pltpu.roll(x, 1, 0))
    xp1 = jnp.where(row == tm - 1, next_row, pltpu.roll(x, tm - 1, 0))
    h = _silu(cw[0:1] * xm1 + cw[1:2] * x + cw[2:3] * xp1)
    q, k, v = h[:, 0:256], h[:, 256:512], h[:, 512:768]
    q = q * lax.rsqrt(_dot((q * q).astype(BF16), bdm) + EPS) * (HEAD_DIM ** -0.5)
    k = k * lax.rsqrt(_dot((k * k).astype(BF16), bdm) + EPS)

    z = ab + dtb
    softplus = jnp.maximum(z, 0.0) + jnp.log(1.0 + jnp.exp(-jnp.abs(z)))
    t = jnp.where(lane < 2 * N_HEADS, -jnp.exp(alog) * softplus, _sigmoid(ab))
    t = jnp.where(lane < 4 * N_HEADS, t, 0.0)
    rowc = row % c
    pre, suf = t, t
    step = 1
    while step < c:
        pre = pre + jnp.where(rowc >= step, pltpu.roll(pre, step, 0), 0.0)
        suf = suf + jnp.where(rowc < c - step, pltpu.roll(suf, tm - step, 0), 0.0)
        step *= 2
    t = jnp.where(lane < N_HEADS, pre, jnp.where(lane < 2 * N_HEADS, suf, t))
    hi = t.astype(BF16).astype(F32)
    rem = t - hi
    mid = rem.astype(BF16).astype(F32)
    comb = (hi + pltpu.roll(mid, 4 * N_HEADS, 1) + pltpu.roll(rem - mid, 8 * N_HEADS, 1)).astype(BF16)
    ex = _dot(comb, expand)
    return q, k, v, [ex[:, 0:256], ex[:, 256:512]], [ex[:, 512:768], ex[:, 768:1024]]


def _dn_local(qkv, ab, cw, alog, dtb, bdm, expand, stm):
    bsz, n, _ = qkv.shape
    tm = TOKEN_TILE
    nt = n // tm
    r8 = tm // 8
    nb = 2 if bsz % 2 == 0 else 1
    tok = lambda w_: pl.BlockSpec((nb, tm, w_), lambda b, i: (b, i, 0))
    outs = [jax.ShapeDtypeStruct((2, bsz, n, PK_WIDTH), BF16), jax.ShapeDtypeStruct((2, bsz, n // 8, 256), F32)]
    return pl.pallas_call(
        _dn_local_kernel,
        grid=(bsz // nb, nt),
        in_specs=[
            tok(768),
            pl.BlockSpec((nb, 8, 768), lambda b, i: (b, jnp.maximum(i * r8 - 1, 0), 0)),
            pl.BlockSpec((nb, 8, 768), lambda b, i: (b, jnp.minimum((i + 1) * r8, nt * r8 - 1), 0)),
            tok(128),
        ] + [_resident(a.shape) for a in (cw, alog, dtb, bdm, expand, stm)],
        out_specs=[pl.BlockSpec((2, nb, tm, PK_WIDTH), lambda b, i: (0, b, i, 0)),
                   pl.BlockSpec((2, nb, tm // 8, 256), lambda b, i: (0, b, i, 0))],
        out_shape=outs,
        compiler_params=_cparams(("parallel", "arbitrary")),
        name="dn_local",
    )(qkv, qkv, qkv, ab, cw, alog, dtb, bdm, expand, stm)


def _dn_scan_kernel(pf_ref, ef_ref, pb_ref, eb_ref, bdm_ref, of_ref, ob_ref, s_ref):
    c = DN_CHUNK
    nch = TOKEN_TILE // c
    nb = of_ref.shape[0]

    @pl.when(pl.program_id(1) == 0)
    def _():
        s_ref[...] = jnp.zeros_like(s_ref)

    bdm = bdm_ref[...]
    bdm_f = bdm.astype(F32)
    ins = ((pf_ref, ef_ref), (pb_ref, eb_ref))
    outs = (of_ref, ob_ref)
    for step in range(nch):
        chains = [(d, bi, step if d == 0 else nch - 1 - step) for bi in range(nb) for d in range(2)]
        rows = lambda ci: slice(ci * c, (ci + 1) * c)
        ws, vb = {}, {}
        for d, bi, ci in chains:
            p_ref = ins[d][0]
            lhs = jnp.concatenate([p_ref[0, bi, rows(ci), PK_W], p_ref[0, bi, rows(ci), PK_QG]], axis=0)
            ws[d, bi] = _dot(lhs, s_ref[d, bi].astype(BF16))
        for d, bi, ci in chains:
            p_ref = ins[d][0]
            vb[d, bi] = (p_ref[0, bi, rows(ci), PK_U].astype(F32) - ws[d, bi][:c]).astype(BF16)
            outs[d][bi, rows(ci), :] = ws[d, bi][c:] + _dot(p_ref[0, bi, rows(ci), PK_IN], _bd(vb[d, bi], bdm))
        for d, bi, ci in chains:
            p_ref, e_ref = ins[d]
            egl = e_ref[0, bi, ci * 8:ci * 8 + 1, :]
            s_ref[d, bi] = s_ref[d, bi] * egl + _dot_tn(p_ref[0, bi, rows(ci), PK_KD], vb[d, bi]) * bdm_f


def _dn_scan(pk, egl, bdm):
    _, bsz, n, _ = pk.shape
    tm = TOKEN_TILE
    nt = n // tm
    nb = 4 if bsz % 4 == 0 else (2 if bsz % 2 == 0 else 1)
    bidx = lambda s: jnp.where(s == 0, 0, nt - s)
    f4 = lambda r, w_: pl.BlockSpec((1, nb, r, w_), lambda b, s: (0, b, s, 0))
    b4 = lambda r, w_: pl.BlockSpec((1, nb, r, w_), lambda b, s: (1, b, bidx(s), 0))
    return pl.pallas_call(
        _dn_scan_kernel,
        grid=(bsz // nb, nt),
        in_specs=[f4(tm, PK_WIDTH), f4(tm // 8, 256), b4(tm, PK_WIDTH), b4(tm // 8, 256), _resident(bdm.shape)],
        out_specs=[pl.BlockSpec((nb, tm, 256), lambda b, s: (b, s, 0)),
                   pl.BlockSpec((nb, tm, 256), lambda b, s: (b, bidx(s), 0))],
        out_shape=[jax.ShapeDtypeStruct((bsz, n, 256), F32)] * 2,
        scratch_shapes=[pltpu.VMEM((2, nb, GROUP_W, GROUP_W), F32)],
        compiler_params=_cparams(("parallel", "arbitrary")),
        name="dn_scan",
    )(pk, egl, pk, egl, bdm)


def _half_swap(n):
    q = n // 4
    i = np.arange(n)
    return np.where((i % (2 * q)) < q, i + q, i - q)


def _inproj_columns():
    off = np.cumsum((0,) + IN_SIZES)
    z = IN_PROJ
    sw64, sw32 = _half_swap(64), _half_swap(32)
    gg, kk, dd = np.meshgrid(np.arange(2), np.arange(2), np.arange(64), indexing="ij")
    swa_q = (off[0] + kk * 128 + gg * 64 + dd).reshape(-1)
    swa_qs = (off[0] + kk * 128 + gg * 64 + sw64[dd]).reshape(-1)
    kk2, dd2 = np.meshgrid(np.arange(2), np.arange(64), indexing="ij")
    swa_k = (off[1] + kk2 * 64 + dd2).reshape(-1)
    swa_ks = (off[1] + kk2 * 64 + sw64[dd2]).reshape(-1)
    zeros = lambda n: np.full((n,), z)
    cols = np.concatenate([
        swa_q, swa_qs, swa_k, swa_ks, off[2] + np.arange(128),
        off[3] + np.arange(768), off[4] + np.arange(256),
        off[5] + np.arange(8), off[6] + np.arange(8), zeros(112),
        off[7] + np.arange(256), off[8] + np.arange(128),
        zeros(64), off[9] + np.arange(32), zeros(32),
        zeros(64), off[9] + sw32, zeros(32),
        off[10] + np.arange(256), off[11] + np.arange(256), off[12] + np.arange(256),
    ])
    assert cols.shape == (P_TOTAL,)
    return cols


def _gather_cols(w, cols):
    wz = jnp.concatenate([w, jnp.zeros(w.shape[:-1] + (1,), w.dtype)], axis=-1)
    return jnp.take(wz, jnp.asarray(cols), axis=-1)


def _mla_columns():
    dqk = MLA_NOPE + MLA_ROPE
    zq, zkv = N_HEADS * dqk, N_HEADS * (MLA_NOPE + MLA_V)
    sw32 = _half_swap(32)
    wq, wqs, wk, wv = [], [], [], []
    for h in range(N_HEADS):
        wq += [h * dqk + np.arange(dqk), np.full((32,), zq)]
        wqs += [np.full((64,), zq), h * dqk + MLA_NOPE + sw32, np.full((32,), zq)]
        wk += [h * 128 + np.arange(64), np.full((64,), zkv)]
        wv += [h * 128 + 64 + np.arange(64), np.full((64,), zkv)]
    return tuple(np.concatenate(c) for c in (wq, wqs, wk, wv))


def _wout_rows():
    z = 4 * GROUP_W
    gg, kk, dd = np.meshgrid(np.arange(2), np.arange(2), np.arange(64), indexing="ij")
    ya = (kk * 128 + gg * 64 + dd).reshape(-1)
    ym = np.concatenate([np.concatenate([2 * GROUP_W + h * 64 + np.arange(64), np.full((64,), z)])
                         for h in range(N_HEADS)])
    return np.concatenate([ya, GROUP_W + np.arange(256), ym, 3 * GROUP_W + np.arange(256)])


def _rope_tables(seq):
    t = np.arange(seq)
    row = (t // GRID_W).astype(np.float32)
    col = (t % GRID_W).astype(np.float32)

    def cs(n):
        inv = (np.float32(ROPE_THETA) ** (-np.arange(0, n, 2, dtype=np.float32) / np.float32(n))).astype(np.float32)
        ar, ac = row[:, None] * inv[None, :], col[:, None] * inv[None, :]
        cos = np.concatenate([np.cos(ar)] * 2 + [np.cos(ac)] * 2, axis=-1)
        sin = np.concatenate([-np.sin(ar), np.sin(ar), -np.sin(ac), np.sin(ac)], axis=-1)
        return cos.astype(np.float32), sin.astype(np.float32)

    cos64, sin64 = cs(HEAD_DIM // 2)
    cos32, sin32 = cs(MLA_ROPE // 2)
    ones = lambda w_: np.ones((CTX_LEN, w_), np.float32)
    zeros = lambda w_: np.zeros((CTX_LEN, w_), np.float32)
    cos_a = np.concatenate([ones(128), np.concatenate([cos64, cos64], axis=-1)], axis=0)
    sin_a = np.concatenate([zeros(128), np.concatenate([sin64, sin64], axis=-1)], axis=0)
    pad1, pad0 = np.ones((seq, 64), np.float32), np.zeros((seq, 32), np.float32)
    cos_m = np.concatenate([np.concatenate([ones(96), zeros(32)], axis=-1),
                            np.concatenate([pad1, cos32, pad0], axis=-1)], axis=0)
    sin_m = np.concatenate([zeros(128),
                            np.concatenate([np.zeros((seq, 64), np.float32), sin32, pad0], axis=-1)], axis=0)
    return tuple(jnp.asarray(a) for a in (cos_a, sin_a, cos_m, sin_m))


def _swa_band():
    i = np.arange(4 * SWA_BLOCK)[:, None] % SWA_BLOCK
    j = np.arange(3 * SWA_BLOCK + CTX_LEN)[None, :]
    ok = (j >= 3 * SWA_BLOCK) | ((j >= i) & (j <= i + 2 * SWA_WINDOW))
    return jnp.asarray(np.where(ok, 0.0, NEG_INF), F32)


def _na_bias_table(rpb):
    w = GRID_W
    qc = np.arange(w)[:, None]
    kc = np.arange(w)[None, :]
    cs = np.clip(qc - NA_KC // 2, 0, w - NA_KC)
    ok = (kc >= cs) & (kc < cs + NA_KC)
    dc = np.clip(kc - qc + NA_KC - 1, 0, 2 * NA_KC - 2)
    onehot = (np.arange(2 * NA_KC - 1)[None, :, None] == dc[:, None, :]).astype(np.float32)
    full = jnp.einsum("hrd,qdk->hrqk", rpb, jnp.asarray(onehot), precision=HIGHEST)
    full = jnp.where(jnp.asarray(ok)[None, None], full, NEG_INF)
    tabs = [jnp.transpose(full[:, NA_KR - 1 - dl:2 * NA_KR - 1 - dl], (0, 2, 1, 3)).reshape(N_HEADS * w, NA_KR * w)
            for dl in range(NA_KR)]
    tabs.append(jnp.full((N_HEADS * w, NA_KR * w), NEG_INF, F32))
    return jnp.stack(tabs)


def _dn_constants():
    c = DN_CHUNK
    hh = np.arange(GROUP_W) // HEAD_DIM
    bd = (hh[:, None] == hh[None, :])
    expand = np.zeros((LANES, 4 * GROUP_W), np.float32)
    for piece in range(3):
        for d in range(2):
            expand[16 * piece + d * 4 + hh, d * GROUP_W + np.arange(GROUP_W)] = 1.0
            expand[16 * piece + 8 + d * 4 + hh, (2 + d) * GROUP_W + np.arange(GROUP_W)] = 1.0
    i = np.arange(c)[:, None]
    jj = (np.arange(GROUP_W) % c)[None, :]
    masks = [(jj <= i), (jj < i), (jj >= i), (jj > i), (jj == i)]
    for d in range(2):
        for lvl in range(6):
            b = 1 << lvl
            same = (i // (2 * b)) == (jj // (2 * b))
            lo, hi = (jj % (2 * b)) < b, (i % (2 * b)) >= b
            masks.append(same & (lo & hi if d == 0 else ~lo & ~hi))
    stm = np.stack(masks).astype(np.float32)
    return jnp.asarray(bd, BF16), jnp.asarray(expand, BF16), jnp.asarray(stm)


def kernel(x, c, ctx, c_ctx, ada_w, ada_b, norm1_g, ffn1_wg, ffn1_wu, ffn1_wd, norm2_g, w_in, swa_sink,
           dn_conv_w, dn_a_log, dn_dt_bias, dn_norm_g, mla_q_norm_g, mla_w_uq, mla_kv_norm_g, mla_w_ukv,
           na_rpb, w_out, norm3_g, ffn2_wg, ffn2_wu, ffn2_wd, final_norm_g):
    bsz, seq, d = x.shape
    depth = ada_w.shape[0]
    assert d == D_MODEL and ctx.shape[1] == CTX_LEN and seq % TOKEN_TILE == 0 and bsz <= CTX_ROW
    assert CTX_ROW % BATCH_ROWS == 0 and CTX_ROW + BATCH_ROWS <= MOD_ROWS

    cvec = jnp.zeros((MOD_ROWS, d), F32).at[:bsz].set(c).at[CTX_ROW:].set(c_ctx)
    mod = _modulation(cvec, ada_w, ada_b).reshape(depth, MOD_ROWS, 3, 3, d)

    cos_a, sin_a, cos_m, sin_m = _rope_tables(seq)
    band = _swa_band()
    bdm, expand, stm = _dn_constants()
    in_cols = _inproj_columns()
    mq, mqs, mk, mv = _mla_columns()
    wo_rows = _wout_rows()
    row2 = lambda v: v.reshape(1, -1)

    h = x
    for i in range(depth):
        last = i == depth - 1
        bf = lambda a: a.astype(BF16)
        h = _ffn(h, mod[i, :, 0], row2(norm1_g[i]), bf(ffn1_wg[i]), bf(ffn1_wu[i]), bf(ffn1_wd[i]),
                 ctx=ctx if i == 0 else None)
        wp = bf(_gather_cols(w_in[i], in_cols))
        mla = (row2(mla_q_norm_g[i]), row2(mla_kv_norm_g[i]),
               bf(_gather_cols(mla_w_uq[i], mq)), bf(_gather_cols(mla_w_uq[i], mqs)),
               bf(_gather_cols(mla_w_ukv[i], mk)), bf(_gather_cols(mla_w_ukv[i], mv)), cos_m, sin_m)
        swa_q, swa_kv, dn_qkv, dn_z, dn_ab, na_q, na_kv, mla_q, mla_k, mla_v = _inproj(
            h, mod[i, :, 1], row2(norm2_g[i]), wp, cos_a, sin_a, mla)

        ya = _swa(swa_q, swa_kv, swa_sink[i], band)
        yn = _na(na_q, na_kv, _na_bias_table(na_rpb[i] * LOG2E))
        ym = _mla_attn(mla_q, mla_k, mla_v, with_ctx=not last)
        pad = lambda v: jnp.zeros((1, LANES), F32).at[0, :2 * N_HEADS].set(v.reshape(-1))
        pk, egl = _dn_local(dn_qkv, dn_ab, dn_conv_w[i], pad(dn_a_log[i]), pad(dn_dt_bias[i]), bdm, expand, stm)
        o_f, o_b = _dn_scan(pk, egl, bdm)

        wo = bf(jnp.take(jnp.concatenate([w_out[i], jnp.zeros((1, d), F32)], axis=0), jnp.asarray(wo_rows), axis=0))
        mix = (ya, o_f, o_b, dn_z, ym, yn, mod[i, :, 1], row2(jnp.tile(dn_norm_g[i], N_HEADS)), bdm, wo)
        h = _ffn(h, mod[i, :, 2], row2(norm3_g[i]), bf(ffn2_wg[i]), bf(ffn2_wu[i]), bf(ffn2_wd[i]), mix=mix,
                 final_g=row2(final_norm_g) if last else None, latent_only=last)
    return h
```

```python
import functools

import numpy as np
import jax
import jax.numpy as jnp
from jax import lax
from jax.experimental import pallas as pl
from jax.experimental.pallas import tpu as pltpu

F32 = jnp.float32
BF16 = jnp.bfloat16
HIGHEST = lax.Precision.HIGHEST

D_MODEL = 1024
CTX_LEN = 256
GRID_W = 64
HEAD_DIM = 64
N_HEADS = 4
GROUP_W = N_HEADS * HEAD_DIM
D_FF = 2816
N_MOD = 9
ROPE_THETA = 10000.0
EPS = 1e-6
NEG_INF = -1e30
LOG2E = 1.4426950408889634
SWA_WINDOW = 128
SWA_BLOCK = 128
DN_CHUNK = 64
DN_LEVEL0 = 5
PK_U, PK_W, PK_QG, PK_KD, PK_IN = (slice(j * 256, (j + 1) * 256) for j in range(5))
PK_WIDTH = 5 * 256
MLA_Q_RANK = 256
MLA_KV_RANK = 128
MLA_NOPE = 64
MLA_ROPE = 32
MLA_V = 64
NA_KR = 8
NA_KC = 16
NA_UNITS = 4
IN_SIZES = (256, 128, 128, 768, 256, 8, 8, 256, 128, 32, 256, 256, 256)
IN_PROJ = sum(IN_SIZES)

LANES = 128
TOKEN_TILE = 256
VMEM_LIMIT = 56 * 1024 * 1024
MOD_ROWS = 16
CTX_ROW = 8
BATCH_ROWS = 2

P_SWA_Q, P_SWA_K, P_SWA_V = 0, 256, 384
P_DN_QKV, P_DN_Z, P_DN_AB = 512, 1280, 1536
P_MLA = 1664
P_NA_Q, P_NA_KV = 2176, 2432
P_TOTAL = 2944


def _cparams(sem):
    return pltpu.CompilerParams(dimension_semantics=sem, vmem_limit_bytes=VMEM_LIMIT)


def _dot(a, b, precision=None):
    return jnp.dot(a, b, preferred_element_type=F32, precision=precision)


def _dot_nt(a, b):
    return lax.dot_general(a, b, (((1,), (1,)), ((), ())), preferred_element_type=F32)


def _dot_tn(a, b):
    return lax.dot_general(a, b, (((0,), (0,)), ((), ())), preferred_element_type=F32)


def _sigmoid(x):
    return 1.0 / (1.0 + jnp.exp(-x))


def _silu(x):
    return x * _sigmoid(x)


def _resident(shape):
    nd = len(shape)
    return pl.BlockSpec(shape, lambda *_: (0,) * nd, pipeline_mode=pl.Buffered(1))


def _mod_kernel(c_ref, w_ref, b_ref, o_ref):
    s = _silu(c_ref[...])
    o_ref[0] = _dot(s, w_ref[0], HIGHEST) + b_ref[0]


def _modulation(cvec, ada_w, ada_b):
    n_layers = ada_w.shape[0]
    d = D_MODEL
    return pl.pallas_call(
        _mod_kernel,
        grid=(n_layers, N_MOD),
        in_specs=[
            pl.BlockSpec((MOD_ROWS, d), lambda l, j: (0, 0)),
            pl.BlockSpec((1, d, d), lambda l, j: (l, 0, j)),
            pl.BlockSpec((1, 1, d), lambda l, j: (l, 0, j)),
        ],
        out_specs=pl.BlockSpec((1, MOD_ROWS, d), lambda l, j: (l, 0, j)),
        out_shape=jax.ShapeDtypeStruct((n_layers, MOD_ROWS, N_MOD * d), F32),
        compiler_params=_cparams(("arbitrary", "arbitrary")),
        name="modulation",
    )(cvec, ada_w, ada_b.reshape(n_layers, 1, N_MOD * d))


def _mod_index(b, i, nb=1):
    return (jnp.where(i == 0, CTX_ROW // nb, b), 0, 0)


def _prenorm(x, g, shift, scale):
    ms = jnp.mean(x * x, axis=-1, keepdims=True)
    return (x * lax.rsqrt(ms + EPS) * g) * (1.0 + scale) + shift


def _group_mean_sq(x, ones_bd):
    return _dot((x * x).astype(BF16), ones_bd) * (1.0 / HEAD_DIM)


def _ffn_kernel(*refs, mix, final, split):
    it = iter(refs)
    x_ref = next(it)
    if split:
        ctx_ref = next(it)
    if mix:
        ya_ref, of_ref, ob_ref, z_ref, ym_ref, yn_ref = (next(it) for _ in range(6))
        ymc_ref = next(it) if mix == "ctx" else None
        mmod_ref, dng_ref, ones_ref, wo_ref = (next(it) for _ in range(4))
    mod_ref, g_ref, wg_ref, wu_ref, wd_ref = (next(it) for _ in range(5))
    if final:
        fg_ref = next(it)
    o_ref = next(it)

    nb, tm, _ = x_ref.shape
    xs, hs = [], []
    for bi in range(nb):
        x = x_ref[bi]
        if split:
            x = jnp.where(pl.program_id(1) == 0, ctx_ref[bi], x)
        if mix:
            o = of_ref[bi] + ob_ref[bi]
            yd = (o * lax.rsqrt(_group_mean_sq(o, ones_ref[...]) + EPS) * dng_ref[...]) * _silu(z_ref[bi])
            ym = ym_ref[bi] if ymc_ref is None else jnp.where(pl.program_id(1) == 0, ymc_ref[bi], ym_ref[bi])
            ycat = jnp.concatenate([ya_ref[bi], yd.astype(BF16), ym, yn_ref[bi]], axis=-1)
            x = x + mmod_ref[bi][2:3] * _dot(ycat, wo_ref[...])
        mod = mod_ref[bi]
        xs.append(x)
        hs.append(_prenorm(x, g_ref[...], mod[0:1], mod[1:2]).astype(BF16))
    h = jnp.concatenate(hs, axis=0)
    a = (_silu(_dot(h, wg_ref[...])) * _dot(h, wu_ref[...])).astype(BF16)
    y = _dot(a, wd_ref[...])
    for bi in range(nb):
        x = xs[bi] + (0.5 * mod_ref[bi][2:3]) * y[bi * tm:(bi + 1) * tm]
        if final:
            ms = jnp.mean(x * x, axis=-1, keepdims=True)
            x = x * lax.rsqrt(ms + EPS) * fg_ref[...]
        o_ref[bi] = x


def _ffn(x, mod, g, wg, wu, wd, mix=None, final_g=None, latent_only=False, ctx=None):
    bsz, n, d = x.shape
    tm = TOKEN_TILE
    nb = BATCH_ROWS if bsz % BATCH_ROWS == 0 else 1
    off = 1 if latent_only else 0
    tiles = n // tm - off + (ctx is not None)
    tok = lambda w: pl.BlockSpec((nb, tm, w), lambda b, i: (b, i + off, 0))
    modspec = pl.BlockSpec((nb, 3, d), lambda b, i: _mod_index(b, i + off, nb))
    if ctx is not None:
        assert mix is None and not latent_only and ctx.shape[1] == tm
        args = [x, ctx]
        specs = [pl.BlockSpec((nb, tm, d), lambda b, i: (b, jnp.maximum(i - 1, 0), 0)),
                 pl.BlockSpec((nb, tm, d), lambda b, i: (b, 0, 0))]
    else:
        args, specs = [x], [tok(d)]
    if mix is not None:
        ya, of, ob, z, (ym, ymc), yn, mmod, dng, ones_bd, wo = mix
        assert (ymc is None) == latent_only
        lat = pl.BlockSpec((nb, tm, ym.shape[-1]), lambda b, i: (b, jnp.maximum(i + off - 1, 0), 0))
        args += [ya, of, ob, z, ym, yn]
        specs += [tok(ya.shape[-1]), tok(of.shape[-1]), tok(ob.shape[-1]), tok(z.shape[-1]), lat, tok(yn.shape[-1])]
        if ymc is not None:
            args.append(ymc)
            specs.append(pl.BlockSpec((nb, tm, ymc.shape[-1]), lambda b, i: (b, 0, 0)))
        args += [mmod, dng, ones_bd, wo]
        specs += [modspec, _resident(dng.shape), _resident(ones_bd.shape), _resident(wo.shape)]
    args += [mod, g, wg, wu, wd]
    specs += [modspec, _resident(g.shape), _resident(wg.shape), _resident(wu.shape), _resident(wd.shape)]
    if final_g is not None:
        args.append(final_g)
        specs.append(_resident(final_g.shape))
    return pl.pallas_call(
        functools.partial(_ffn_kernel, mix=(None if mix is None else ("latent" if latent_only else "ctx")),
                          final=final_g is not None, split=ctx is not None),
        grid=(bsz // nb, tiles),
        in_specs=specs,
        out_specs=pl.BlockSpec((nb, tm, d), lambda b, i: (b, i, 0)),
        out_shape=jax.ShapeDtypeStruct((bsz, tiles * tm, d), F32),
        compiler_params=_cparams(("parallel", "arbitrary")),
        name="ffn_mix" if mix is not None else "ffn",
    )(*args)


def _rope_partner(x, lane, q):
    w = x.shape[-1]
    return jnp.where((lane % (2 * q)) < q, pltpu.roll(x, w - q, 1), pltpu.roll(x, q, 1))


def _inproj_kernel(x_ref, mod_ref, g_ref, w_ref, cos_ref, sin_ref,
                   gq_ref, gkv_ref, wq_ref, wk_ref, wv_ref, vone_ref, cosm_ref, sinm_ref,
                   swaq_ref, swakv_ref, dnqkv_ref, dnz_ref, dnab_ref, naq_ref, nakv_ref,
                   mq_ref, mk_ref, mv_ref):
    nb, tm, _ = x_ref.shape
    g = g_ref[...]
    h = jnp.concatenate([_prenorm(x_ref[bi], g, mod_ref[bi][0:1], mod_ref[bi][1:2]).astype(BF16)
                         for bi in range(nb)], axis=0)
    u_all = _dot(h, w_ref[...])
    cos, sin = cos_ref[...], sin_ref[...]
    cos2 = jnp.concatenate([cos, cos], axis=-1)
    sin2 = jnp.concatenate([sin, sin], axis=-1)
    scale = HEAD_DIM ** -0.5 * LOG2E
    cosm, sinm = cosm_ref[...], sinm_ref[...]
    cos4 = jnp.concatenate([cosm] * N_HEADS, axis=-1)
    sin4 = jnp.concatenate([sinm] * N_HEADS, axis=-1)
    mscale = (MLA_NOPE + MLA_ROPE) ** -0.5 * LOG2E
    lane4 = lax.broadcasted_iota(jnp.int32, (1, N_HEADS * LANES), 1)
    lane2 = lane4[:, :2 * LANES]
    for bi in range(nb):
        u = u_all[bi * tm:(bi + 1) * tm]
        q = u[:, P_SWA_Q:P_SWA_Q + 256]
        swaq_ref[bi] = ((q * cos2 + _rope_partner(q, lane2, HEAD_DIM // 4) * sin2) * scale).astype(BF16)
        k = u[:, P_SWA_K:P_SWA_K + 128]
        k = k * cos + _rope_partner(k, lane2[:, :LANES], HEAD_DIM // 4) * sin
        swakv_ref[bi] = jnp.concatenate([k, u[:, P_SWA_V:P_SWA_V + 128]], axis=-1).astype(BF16)
        dnqkv_ref[bi] = u[:, P_DN_QKV:P_DN_QKV + 768]
        dnz_ref[bi] = u[:, P_DN_Z:P_DN_Z + 256]
        dnab_ref[bi] = u[:, P_DN_AB:P_DN_AB + 128]
        naq_ref[bi] = (u[:, P_NA_Q:P_NA_Q + 256] * scale).astype(BF16)
        nakv_ref[bi] = u[:, P_NA_KV:P_NA_KV + 512].astype(BF16)

        cq, ckv = u[:, P_MLA:P_MLA + 256], u[:, P_MLA + 256:P_MLA + 384]
        kra = u[:, P_MLA + 384:P_MLA + 512]
        krb = _rope_partner(kra, lane2[:, :LANES], MLA_ROPE // 4)
        nq = (cq * lax.rsqrt(jnp.mean(cq * cq, axis=-1, keepdims=True) + EPS) * gq_ref[...]).astype(BF16)
        nkv = (ckv * lax.rsqrt(jnp.mean(ckv * ckv, axis=-1, keepdims=True) + EPS) * gkv_ref[...]).astype(BF16)
        qf = _dot(nq, wq_ref[...])
        mq_ref[bi] = ((qf * cos4 + _rope_partner(qf, lane4, MLA_ROPE // 4) * sin4) * mscale).astype(BF16)
        kr = kra * cosm + krb * sinm
        mk_ref[bi] = (_dot(nkv, wk_ref[...]) + jnp.concatenate([kr] * N_HEADS, axis=-1)).astype(BF16)
        mv_ref[bi] = (_dot(nkv, wv_ref[...]) + vone_ref[...]).astype(BF16)


def _inproj(x, mod, g, w, cos, sin, mla):
    bsz, n, d = x.shape
    tm = TOKEN_TILE
    gq, gkv, wq, wk, wv, cosm, sinm = mla
    one_lane = np.where((np.arange(N_HEADS * LANES) // LANES) % 2 == 0, MLA_V, 0)
    vone = jnp.asarray((np.arange(N_HEADS * LANES) % LANES == one_lane).astype(np.float32)).reshape(1, -1)
    widths = (256, 256, 768, 256, 128, 256, 512)
    dtypes = (BF16, BF16, F32, F32, F32, BF16, BF16)
    nb = BATCH_ROWS if bsz % BATCH_ROWS == 0 else 1
    tok = lambda w_: pl.BlockSpec((nb, tm, w_), lambda b, i: (b, i, 0))
    tab = pl.BlockSpec((tm, LANES), lambda b, i: (i, 0))
    res = [gq, gkv, wq, wk, wv, vone]
    mw = N_HEADS * LANES
    return pl.pallas_call(
        _inproj_kernel,
        grid=(bsz // nb, n // tm),
        in_specs=[tok(d), pl.BlockSpec((nb, 3, d), lambda b, i: _mod_index(b, i, nb)), _resident(g.shape),
                  _resident(w.shape), tab, tab]
                 + [_resident(a.shape) for a in res] + [tab, tab],
        out_specs=[tok(w_) for w_ in widths] + [tok(mw)] * 3,
        out_shape=[jax.ShapeDtypeStruct((bsz, n, w_), dt) for w_, dt in zip(widths, dtypes)]
                  + [jax.ShapeDtypeStruct((bsz, n, mw), BF16)] * 3,
        compiler_params=_cparams(("parallel", "arbitrary")),
        name="inproj",
    )(x, mod, g, w, cos, sin, *res, cosm, sinm)


def _swa_kernel(sink_ref, q_ref, kvp_ref, kvc_ref, kvn_ref, kvx_ref, band_ref, o_ref):
    i = pl.program_id(1)
    nt = pl.num_programs(1)
    blk = SWA_BLOCK
    lane = lax.broadcasted_iota(jnp.int32, (1, LANES), 1)
    hsel = [lane < HEAD_DIM, lane >= HEAD_DIM]
    hmask = [m.astype(F32) for m in hsel]
    q = q_ref[0]
    cur = kvc_ref[0]
    kvx = kvx_ref[0]
    latent = i > 0
    blocks = [(kvp_ref[0], cur[:blk], cur[blk:]), (cur[:blk], cur[blk:], kvn_ref[0])]
    valid = [(latent & (i > 1), latent, latent), (latent, latent, latent & (i < nt - 1))]
    rblk = lax.broadcasted_iota(jnp.int32, (4 * blk, 1), 0) // blk
    sk = jnp.where(rblk == 0, sink_ref[0], jnp.where(rblk == 1, sink_ref[2],
                                                       jnp.where(rblk == 2, sink_ref[1], sink_ref[3]))) * LOG2E
    zero = jnp.zeros((1, LANES), F32)
    s, vcat = [], []
    for u in range(2):
        qu = q[u * blk:(u + 1) * blk]
        qs = jnp.concatenate([jnp.where(hsel[kh], qu[:, g * LANES:(g + 1) * LANES], 0)
                              for g in range(2) for kh in range(2)], axis=0)
        kvs = list(blocks[u]) + [kvx]
        kcat = jnp.concatenate([t[:, :LANES] for t in kvs], axis=0)
        vcat.append(jnp.concatenate([t[:, LANES:] for t in kvs], axis=0))
        pen = jnp.concatenate([zero + jnp.where(ok, 0.0, NEG_INF) for ok in valid[u]] + [zero, zero], axis=-1)
        s.append(_dot_nt(qs, kcat) + band_ref[...] + pen)
    p, l = [], []
    for u in range(2):
        m = jnp.maximum(jnp.max(s[u], axis=-1, keepdims=True), sk)
        pu = jnp.exp2(s[u] - m)
        l.append(jnp.sum(pu, axis=-1, keepdims=True) + jnp.exp2(sk - m))
        p.append(pu.astype(BF16))
    for u in range(2):
        o = _dot(p[u], vcat[u]) * (1.0 / l[u])
        outs = [o[(2 * g) * blk:(2 * g + 1) * blk] * hmask[0] + o[(2 * g + 1) * blk:(2 * g + 2) * blk] * hmask[1]
                for g in range(2)]
        o_ref[0, u * blk:(u + 1) * blk, :] = jnp.concatenate(outs, axis=-1).astype(o_ref.dtype)


def _swa(q, kv, sink, band):
    bsz, n, _ = q.shape
    blk = SWA_BLOCK
    tm = 2 * blk
    first = CTX_LEN // blk
    last = n // blk - 1
    return pl.pallas_call(
        _swa_kernel,
        grid=(bsz, n // tm),
        in_specs=[
            pl.BlockSpec(memory_space=pltpu.SMEM),
            pl.BlockSpec((1, tm, 256), lambda b, i: (b, i, 0)),
            pl.BlockSpec((1, blk, 256), lambda b, i: (b, jnp.clip(2 * i - 1, first, last), 0)),
            pl.BlockSpec((1, tm, 256), lambda b, i: (b, jnp.maximum(i, 1), 0)),
            pl.BlockSpec((1, blk, 256), lambda b, i: (b, jnp.clip(2 * i + 2, first, last), 0)),
            pl.BlockSpec((1, CTX_LEN, 256), lambda b, i: (b, 0, 0)),
            _resident(band.shape),
        ],
        out_specs=pl.BlockSpec((1, tm, 256), lambda b, i: (b, i, 0)),
        out_shape=jax.ShapeDtypeStruct((bsz, n, 256), BF16),
        compiler_params=_cparams(("parallel", "arbitrary")),
        name="swa",
    )(sink, q, kv, kv, kv, kv, band)


def _na_kernel(q_ref, kv_ref, tab_ref, o_ref, *, rows):
    i = pl.program_id(1)
    w = GRID_W
    nctx = CTX_LEN // w
    nloc = NA_KR * w
    lane = lax.broadcasted_iota(jnp.int32, (1, GROUP_W), 1) // HEAD_DIM
    hsel = [lane == h for h in range(N_HEADS)]
    hmask = [m.astype(F32) for m in hsel]
    kv_ctx = kv_ref[0, pl.ds(0, CTX_LEN), :]
    s, vcat = [], []
    for u in range(NA_UNITS):
        j = NA_UNITS * i + u
        r = jnp.maximum(j - nctx, 0)
        rs = jnp.clip(r - NA_KR // 2, 0, rows - NA_KR)
        tab_i = jnp.where(j < nctx, NA_KR, r - rs)
        qu = q_ref[0, u * w:(u + 1) * w, :]
        qs = jnp.concatenate([jnp.where(hsel[h], qu, 0) for h in range(N_HEADS)], axis=0)
        start = pl.multiple_of(CTX_LEN + rs * w, w)
        kv_loc = kv_ref[0, pl.ds(start, nloc), :]
        kcat = jnp.concatenate([kv_loc[:, :GROUP_W], kv_ctx[:, :GROUP_W]], axis=0)
        vcat.append(jnp.concatenate([kv_loc[:, GROUP_W:], kv_ctx[:, GROUP_W:]], axis=0))
        su = _dot_nt(qs, kcat)
        s.append(jnp.concatenate([su[:, :nloc] + tab_ref[tab_i], su[:, nloc:]], axis=-1))
    p, l = [], []
    for u in range(NA_UNITS):
        pu = jnp.exp2(s[u] - jnp.max(s[u], axis=-1, keepdims=True))
        l.append(jnp.sum(pu, axis=-1, keepdims=True))
        p.append(pu.astype(BF16))
    for u in range(NA_UNITS):
        o = _dot(p[u], vcat[u]) * (1.0 / l[u])
        acc = o[0:w] * hmask[0]
        for h in range(1, N_HEADS):
            acc = acc + o[h * w:(h + 1) * w] * hmask[h]
        o_ref[0, u * w:(u + 1) * w, :] = acc.astype(o_ref.dtype)


def _na(q, kv, tab):
    bsz, n, _ = q.shape
    w = GRID_W
    rows = (n - CTX_LEN) // w
    assert rows >= NA_KR
    return pl.pallas_call(
        functools.partial(_na_kernel, rows=rows),
        grid=(bsz, n // (NA_UNITS * w)),
        in_specs=[
            pl.BlockSpec((1, NA_UNITS * w, 256), lambda b, i: (b, i, 0)),
            pl.BlockSpec((1, n, 512), lambda b, i: (b, 0, 0)),
            _resident(tab.shape),
        ],
        out_specs=pl.BlockSpec((1, NA_UNITS * w, 256), lambda b, i: (b, i, 0)),
        out_shape=jax.ShapeDtypeStruct((bsz, n, 256), BF16),
        compiler_params=_cparams(("parallel", "arbitrary")),
        name="na",
    )(q, kv, tab)


def _mla_attn_kernel(*refs, nkeys):
    *q_refs, k_ref, v_ref, o_ref = refs
    tq = TOKEN_TILE
    units = [(r, h) for r in range(len(q_refs)) for h in range(N_HEADS)]
    hl = lambda h: slice(h * LANES, (h + 1) * LANES)
    low = lax.broadcasted_iota(jnp.int32, (1, LANES), 1) < MLA_V
    s, p, even = {}, {}, {}
    for t in range(len(units) + 2):
        if t < len(units):
            r, h = units[t]
            s[t] = _dot_nt(q_refs[r][0, :, hl(h)], k_ref[0, pl.ds(0, nkeys), hl(h)])
        if 0 <= t - 1 < len(units):
            sh = s.pop(t - 1)
            p[t - 1] = jnp.exp2(sh - jnp.max(sh, axis=-1, keepdims=True)).astype(BF16)
        if 0 <= t - 2 < len(units):
            r, h = units[t - 2]
            o = _dot(p.pop(t - 2), v_ref[0, pl.ds(0, nkeys), hl(h)])
            if h % 2 == 0:
                even[r] = o * (1.0 / o[:, MLA_V:MLA_V + 1])
            else:
                pair = jnp.where(low, even.pop(r), o * (1.0 / o[:, 0:1]))
                o_ref[0, r * tq:(r + 1) * tq, hl(h // 2)] = pair.astype(o_ref.dtype)


def _mla_attn(q, k, v, with_ctx):
    bsz, n, w = q.shape
    wo = GROUP_W
    tq = TOKEN_TILE
    pairs = (n // tq - 1) // 2
    assert 2 * pairs + 1 == n // tq
    whole = pl.BlockSpec((1, n, w), lambda b, i: (b, 0, 0))
    lat = pl.pallas_call(
        functools.partial(_mla_attn_kernel, nkeys=n),
        grid=(bsz, pairs),
        in_specs=[pl.BlockSpec((1, tq, w), lambda b, i: (b, 2 * i + 1, 0)),
                  pl.BlockSpec((1, tq, w), lambda b, i: (b, 2 * i + 2, 0)), whole, whole],
        out_specs=pl.BlockSpec((1, 2 * tq, wo), lambda b, i: (b, i, 0)),
        out_shape=jax.ShapeDtypeStruct((bsz, n - tq, wo), BF16),
        compiler_params=_cparams(("parallel", "arbitrary")),
        name="mla_attn",
    )(q, q, k, v)
    if not with_ctx:
        return lat, None
    first = pl.BlockSpec((1, tq, w), lambda b: (b, 0, 0))
    ctx = pl.pallas_call(
        functools.partial(_mla_attn_kernel, nkeys=CTX_LEN),
        grid=(bsz,),
        in_specs=[first, first, first],
        out_specs=pl.BlockSpec((1, tq, wo), lambda b: (b, 0, 0)),
        out_shape=jax.ShapeDtypeStruct((bsz, tq, wo), BF16),
        compiler_params=_cparams(("parallel",)),
        name="mla_attn_ctx",
    )(q, k, v)
    return lat, ctx


def _bd(m, bd_mask):
    return jnp.concatenate([m] * N_HEADS, axis=0) * bd_mask


def _dn_local_kernel(x_ref, xp_ref, xn_ref, ab_ref, cw_ref, alog_ref, dtb_ref, bdm_ref, exp_ref, stm_ref,
                     pk_ref, egl_ref):
    i = pl.program_id(1)
    nt = pl.num_programs(1)
    tm = TOKEN_TILE
    c = DN_CHUNK
    nch = tm // c
    nb = x_ref.shape[0]
    row = lax.broadcasted_iota(jnp.int32, (tm, 1), 0)
    lane = lax.broadcasted_iota(jnp.int32, (1, LANES), 1)
    bdm = bdm_ref[...]
    fronts = [_dn_front(x_ref[bi], xp_ref[bi][7:8], xn_ref[bi][0:1], ab_ref[bi], cw_ref[...], alog_ref[...],
                        dtb_ref[...], bdm, exp_ref[...], row, lane, i, nt) for bi in range(nb)]

    eye = stm_ref[4]
    tile_b = eye.astype(BF16)
    chains = [(bi, d, ci) for bi in range(nb) for ci in range(nch) for d in range(2)]
    rows = lambda ci: slice(ci * c, (ci + 1) * c)
    kq = {}
    for bi in range(nb):
        q, k, _, _, bexps = fronts[bi]
        for ci in range(nch):
            kc = k[rows(ci)]
            kt_bd = _dot_tn(kc.astype(BF16), tile_b).astype(BF16) * bdm
            lhs = jnp.concatenate([kc * bexps[0][rows(ci)], kc * bexps[1][rows(ci)], q[rows(ci)]], axis=0)
            kq[bi, ci] = _dot(lhs.astype(BF16), kt_bd)
    lbs, xinvs = {}, {}
    for bi, d, ci in chains:
        q, k, _, gcols, _ = fronts[bi]
        sl = rows(ci)
        gcol = gcols[d][sl]
        grow = jnp.sum(gcol * eye, axis=0, keepdims=True)
        incl, strict = stm_ref[2 * d], stm_ref[2 * d + 1]
        dec_incl = jnp.exp((gcol - grow) * incl) * incl
        lmat = kq[bi, ci][d * c:(d + 1) * c] * (dec_incl * strict)
        egc = jnp.exp(gcol)
        glast = gcol[c - 1:c] if d == 0 else gcol[0:1]
        pk_ref[d, bi, sl, PK_IN] = (kq[bi, ci][2 * c:3 * c] * dec_incl).astype(BF16)
        pk_ref[d, bi, sl, PK_QG] = (q[sl] * egc).astype(BF16)
        pk_ref[d, bi, sl, PK_KD] = (k[sl] * jnp.exp(glast - gcol)).astype(BF16)
        egl_ref[d, bi, ci * 8:(ci + 1) * 8, :] = jnp.broadcast_to(jnp.exp(glast), (8, GROUP_W))
        lbs[bi, d, ci] = lmat.astype(BF16)
        xinvs[bi, d, ci] = eye - lmat * stm_ref[DN_LEVEL0 + 6 * d]
    for lvl in range(1, 6):
        ys = {}
        for ch in chains:
            off = _bd(lbs[ch] * stm_ref[DN_LEVEL0 + 6 * ch[1] + lvl].astype(BF16), bdm)
            ys[ch] = _dot(xinvs[ch].astype(BF16), off)
        for ch in chains:
            xinvs[ch] = xinvs[ch] - _dot(ys[ch].astype(BF16), _bd(xinvs[ch].astype(BF16), bdm))
    for bi, d, ci in chains:
        _, k, v, gcols, bexps = fronts[bi]
        sl = rows(ci)
        xb = xinvs[bi, d, ci].astype(BF16)
        kb = k[sl] * bexps[d][sl]
        pk_ref[d, bi, sl, PK_U] = _dot(xb, _bd((v[sl] * bexps[d][sl]).astype(BF16), bdm)).astype(BF16)
        pk_ref[d, bi, sl, PK_W] = _dot(xb, _bd((kb * jnp.exp(gcols[d][sl])).astype(BF16), bdm)).astype(BF16)


def _dn_front(x, prev_row, next_row, ab, cw, alog, dtb, bdm, expand, row, lane, i, nt):
    tm = TOKEN_TILE
    c = DN_CHUNK
    prev_row = prev_row * (i > 1).astype(F32)
    next_row = next_row * ((i > 0) & (i < nt - 1)).astype(F32)
    xm1 = jnp.where(row == 0, prev_row, pltpu.roll(x, 1, 0))
    xp1 = jnp.where(row == tm - 1, next_row, pltpu.roll(x, tm - 1, 0))
    h = _silu(cw[0:1] * xm1 + cw[1:2] * x + cw[2:3] * xp1)
    q, k, v = h[:, 0:256], h[:, 256:512], h[:, 512:768]
    q = q * lax.rsqrt(_dot((q * q).astype(BF16), bdm) + EPS) * (HEAD_DIM ** -0.5)
    k = k * lax.rsqrt(_dot((k * k).astype(BF16), bdm) + EPS)

    z = ab + dtb
    softplus = jnp.maximum(z, 0.0) + jnp.log(1.0 + jnp.exp(-jnp.abs(z)))
    t = jnp.where(lane < 2 * N_HEADS, -jnp.exp(alog) * softplus, _sigmoid(ab))
    t = jnp.where(lane < 4 * N_HEADS, t, 0.0)
    rowc = row % c
    pre, suf = t, t
    step = 1
    while step < c:
        pre = pre + jnp.where(rowc >= step, pltpu.roll(pre, step, 0), 0.0)
        suf = suf + jnp.where(rowc < c - step, pltpu.roll(suf, tm - step, 0), 0.0)
        step *= 2
    t = jnp.where(lane < N_HEADS, pre, jnp.where(lane < 2 * N_HEADS, suf, t))
    hi = t.astype(BF16).astype(F32)
    rem = t - hi
    mid = rem.astype(BF16).astype(F32)
    comb = (hi + pltpu.roll(mid, 4 * N_HEADS, 1) + pltpu.roll(rem - mid, 8 * N_HEADS, 1)).astype(BF16)
    ex = _dot(comb, expand)
    return q, k, v, [ex[:, 0:256], ex[:, 256:512]], [ex[:, 512:768], ex[:, 768:1024]]


def _dn_local(qkv, ab, cw, alog, dtb, bdm, expand, stm):
    bsz, n, _ = qkv.shape
    tm = TOKEN_TILE
    nt = n // tm
    r8 = tm // 8
    nb = 2 if bsz % 2 == 0 else 1
    tok = lambda w_: pl.BlockSpec((nb, tm, w_), lambda b, i: (b, i, 0))
    outs = [jax.ShapeDtypeStruct((2, bsz, n, PK_WIDTH), BF16), jax.ShapeDtypeStruct((2, bsz, n // 8, 256), F32)]
    return pl.pallas_call(
        _dn_local_kernel,
        grid=(bsz // nb, nt),
        in_specs=[
            tok(768),
            pl.BlockSpec((nb, 8, 768), lambda b, i: (b, jnp.maximum(i * r8 - 1, 0), 0)),
            pl.BlockSpec((nb, 8, 768), lambda b, i: (b, jnp.minimum((i + 1) * r8, nt * r8 - 1), 0)),
            tok(128),
        ] + [_resident(a.shape) for a in (cw, alog, dtb, bdm, expand, stm)],
        out_specs=[pl.BlockSpec((2, nb, tm, PK_WIDTH), lambda b, i: (0, b, i, 0)),
                   pl.BlockSpec((2, nb, tm // 8, 256), lambda b, i: (0, b, i, 0))],
        out_shape=outs,
        compiler_params=_cparams(("parallel", "arbitrary")),
        name="dn_local",
    )(qkv, qkv, qkv, ab, cw, alog, dtb, bdm, expand, stm)


def _dn_scan_kernel(pf_ref, ef_ref, pb_ref, eb_ref, bdm_ref, of_ref, ob_ref, s_ref):
    c = DN_CHUNK
    nch = TOKEN_TILE // c
    nb = of_ref.shape[0]

    @pl.when(pl.program_id(1) == 0)
    def _():
        s_ref[...] = jnp.zeros_like(s_ref)

    bdm = bdm_ref[...]
    bdm_f = bdm.astype(F32)
    ins = ((pf_ref, ef_ref), (pb_ref, eb_ref))
    outs = (of_ref, ob_ref)
    for step in range(nch):
        chains = [(d, bi, step if d == 0 else nch - 1 - step) for bi in range(nb) for d in range(2)]
        rows = lambda ci: slice(ci * c, (ci + 1) * c)
        ws, vb = {}, {}
        for d, bi, ci in chains:
            p_ref = ins[d][0]
            lhs = jnp.concatenate([p_ref[0, bi, rows(ci), PK_W], p_ref[0, bi, rows(ci), PK_QG]], axis=0)
            ws[d, bi] = _dot(lhs, s_ref[d, bi].astype(BF16))
        for d, bi, ci in chains:
            p_ref = ins[d][0]
            vb[d, bi] = (p_ref[0, bi, rows(ci), PK_U].astype(F32) - ws[d, bi][:c]).astype(BF16)
            outs[d][bi, rows(ci), :] = ws[d, bi][c:] + _dot(p_ref[0, bi, rows(ci), PK_IN], _bd(vb[d, bi], bdm))
        for d, bi, ci in chains:
            p_ref, e_ref = ins[d]
            egl = e_ref[0, bi, ci * 8:ci * 8 + 1, :]
            s_ref[d, bi] = s_ref[d, bi] * egl + _dot_tn(p_ref[0, bi, rows(ci), PK_KD], vb[d, bi]) * bdm_f


def _dn_scan(pk, egl, bdm):
    _, bsz, n, _ = pk.shape
    tm = TOKEN_TILE
    nt = n // tm
    nb = 4 if bsz % 4 == 0 else (2 if bsz % 2 == 0 else 1)
    bidx = lambda s: jnp.where(s == 0, 0, nt - s)
    f4 = lambda r, w_: pl.BlockSpec((1, nb, r, w_), lambda b, s: (0, b, s, 0))
    b4 = lambda r, w_: pl.BlockSpec((1, nb, r, w_), lambda b, s: (1, b, bidx(s), 0))
    return pl.pallas_call(
        _dn_scan_kernel,
        grid=(bsz // nb, nt),
        in_specs=[f4(tm, PK_WIDTH), f4(tm // 8, 256), b4(tm, PK_WIDTH), b4(tm // 8, 256), _resident(bdm.shape)],
        out_specs=[pl.BlockSpec((nb, tm, 256), lambda b, s: (b, s, 0)),
                   pl.BlockSpec((nb, tm, 256), lambda b, s: (b, bidx(s), 0))],
        out_shape=[jax.ShapeDtypeStruct((bsz, n, 256), F32)] * 2,
        scratch_shapes=[pltpu.VMEM((2, nb, GROUP_W, GROUP_W), F32)],
        compiler_params=_cparams(("parallel", "arbitrary")),
        name="dn_scan",
    )(pk, egl, pk, egl, bdm)


def _inproj_columns():
    off = np.cumsum((0,) + IN_SIZES)
    z = IN_PROJ
    gg, kk, dd = np.meshgrid(np.arange(2), np.arange(2), np.arange(64), indexing="ij")
    swa_q = (off[0] + kk * 128 + gg * 64 + dd).reshape(-1)
    zeros = lambda n: np.full((n,), z)
    cols = np.concatenate([
        swa_q, off[1] + np.arange(128), off[2] + np.arange(128),
        off[3] + np.arange(768), off[4] + np.arange(256),
        off[5] + np.arange(8), off[6] + np.arange(8), zeros(112),
        off[7] + np.arange(256), off[8] + np.arange(128),
        zeros(64), off[9] + np.arange(32), zeros(32),
        off[10] + np.arange(256), off[11] + np.arange(256), off[12] + np.arange(256),
    ])
    assert cols.shape == (P_TOTAL,)
    return cols


def _gather_cols(w, cols):
    wz = jnp.concatenate([w, jnp.zeros(w.shape[:-1] + (1,), w.dtype)], axis=-1)
    return jnp.take(wz, jnp.asarray(cols), axis=-1)


def _mla_columns():
    dqk = MLA_NOPE + MLA_ROPE
    zq, zkv = N_HEADS * dqk, N_HEADS * (MLA_NOPE + MLA_V)
    wq, wk, wv = [], [], []
    for h in range(N_HEADS):
        wq += [h * dqk + np.arange(dqk), np.full((32,), zq)]
        wk += [h * 128 + np.arange(64), np.full((64,), zkv)]
        vcols, pad = h * 128 + 64 + np.arange(64), np.full((64,), zkv)
        wv += [vcols, pad] if h % 2 == 0 else [pad, vcols]
    return tuple(np.concatenate(c) for c in (wq, wk, wv))


def _wout_rows():
    gg, kk, dd = np.meshgrid(np.arange(2), np.arange(2), np.arange(64), indexing="ij")
    ya = (kk * 128 + gg * 64 + dd).reshape(-1)
    return np.concatenate([ya, GROUP_W + np.arange(3 * GROUP_W)])


def _rope_tables(seq):
    t = np.arange(seq)
    row = (t // GRID_W).astype(np.float32)
    col = (t % GRID_W).astype(np.float32)

    def cs(n):
        inv = (np.float32(ROPE_THETA) ** (-np.arange(0, n, 2, dtype=np.float32) / np.float32(n))).astype(np.float32)
        ar, ac = row[:, None] * inv[None, :], col[:, None] * inv[None, :]
        cos = np.concatenate([np.cos(ar)] * 2 + [np.cos(ac)] * 2, axis=-1)
        sin = np.concatenate([-np.sin(ar), np.sin(ar), -np.sin(ac), np.sin(ac)], axis=-1)
        return cos.astype(np.float32), sin.astype(np.float32)

    cos64, sin64 = cs(HEAD_DIM // 2)
    cos32, sin32 = cs(MLA_ROPE // 2)
    ones = lambda w_: np.ones((CTX_LEN, w_), np.float32)
    zeros = lambda w_: np.zeros((CTX_LEN, w_), np.float32)
    cos_a = np.concatenate([ones(128), np.concatenate([cos64, cos64], axis=-1)], axis=0)
    sin_a = np.concatenate([zeros(128), np.concatenate([sin64, sin64], axis=-1)], axis=0)
    pad1, pad0 = np.ones((seq, 64), np.float32), np.zeros((seq, 32), np.float32)
    cos_m = np.concatenate([np.concatenate([ones(96), zeros(32)], axis=-1),
                            np.concatenate([pad1, cos32, pad0], axis=-1)], axis=0)
    sin_m = np.concatenate([zeros(128),
                            np.concatenate([np.zeros((seq, 64), np.float32), sin32, pad0], axis=-1)], axis=0)
    return tuple(jnp.asarray(a) for a in (cos_a, sin_a, cos_m, sin_m))


def _swa_band():
    i = np.arange(4 * SWA_BLOCK)[:, None] % SWA_BLOCK
    j = np.arange(3 * SWA_BLOCK + CTX_LEN)[None, :]
    ok = (j >= 3 * SWA_BLOCK) | ((j >= i) & (j <= i + 2 * SWA_WINDOW))
    return jnp.asarray(np.where(ok, 0.0, NEG_INF), F32)


def _na_bias_table(rpb):
    w = GRID_W
    qc = np.arange(w)[:, None]
    kc = np.arange(w)[None, :]
    cs = np.clip(qc - NA_KC // 2, 0, w - NA_KC)
    ok = (kc >= cs) & (kc < cs + NA_KC)
    dc = np.clip(kc - qc + NA_KC - 1, 0, 2 * NA_KC - 2)
    onehot = (np.arange(2 * NA_KC - 1)[None, :, None] == dc[:, None, :]).astype(np.float32)
    full = jnp.einsum("hrd,qdk->hrqk", rpb, jnp.asarray(onehot), precision=HIGHEST)
    full = jnp.where(jnp.asarray(ok)[None, None], full, NEG_INF)
    tabs = [jnp.transpose(full[:, NA_KR - 1 - dl:2 * NA_KR - 1 - dl], (0, 2, 1, 3)).reshape(N_HEADS * w, NA_KR * w)
            for dl in range(NA_KR)]
    tabs.append(jnp.full((N_HEADS * w, NA_KR * w), NEG_INF, F32))
    return jnp.stack(tabs)


def _dn_constants():
    c = DN_CHUNK
    hh = np.arange(GROUP_W) // HEAD_DIM
    bd = (hh[:, None] == hh[None, :])
    expand = np.zeros((LANES, 4 * GROUP_W), np.float32)
    for piece in range(3):
        for d in range(2):
            expand[16 * piece + d * 4 + hh, d * GROUP_W + np.arange(GROUP_W)] = 1.0
            expand[16 * piece + 8 + d * 4 + hh, (2 + d) * GROUP_W + np.arange(GROUP_W)] = 1.0
    i = np.arange(c)[:, None]
    jj = (np.arange(GROUP_W) % c)[None, :]
    masks = [(jj <= i), (jj < i), (jj >= i), (jj > i), (jj == i)]
    for d in range(2):
        for lvl in range(6):
            b = 1 << lvl
            same = (i // (2 * b)) == (jj // (2 * b))
            lo, hi = (jj % (2 * b)) < b, (i % (2 * b)) >= b
            masks.append(same & (lo & hi if d == 0 else ~lo & ~hi))
    stm = np.stack(masks).astype(np.float32)
    return jnp.asarray(bd, BF16), jnp.asarray(expand, BF16), jnp.asarray(stm)


def kernel(x, c, ctx, c_ctx, ada_w, ada_b, norm1_g, ffn1_wg, ffn1_wu, ffn1_wd, norm2_g, w_in, swa_sink,
           dn_conv_w, dn_a_log, dn_dt_bias, dn_norm_g, mla_q_norm_g, mla_w_uq, mla_kv_norm_g, mla_w_ukv,
           na_rpb, w_out, norm3_g, ffn2_wg, ffn2_wu, ffn2_wd, final_norm_g):
    bsz, seq, d = x.shape
    depth = ada_w.shape[0]
    assert d == D_MODEL and ctx.shape[1] == CTX_LEN and seq % TOKEN_TILE == 0 and bsz <= CTX_ROW
    assert CTX_ROW % BATCH_ROWS == 0 and CTX_ROW + BATCH_ROWS <= MOD_ROWS

    cvec = jnp.zeros((MOD_ROWS, d), F32).at[:bsz].set(c).at[CTX_ROW:].set(c_ctx)
    mod = _modulation(cvec, ada_w, ada_b).reshape(depth, MOD_ROWS, 3, 3, d)

    cos_a, sin_a, cos_m, sin_m = _rope_tables(seq)
    band = _swa_band()
    bdm, expand, stm = _dn_constants()
    in_cols = _inproj_columns()
    mq, mk, mv = _mla_columns()
    wo_rows = _wout_rows()
    row2 = lambda v: v.reshape(1, -1)

    h = x
    for i in range(depth):
        last = i == depth - 1
        bf = lambda a: a.astype(BF16)
        h = _ffn(h, mod[i, :, 0], row2(norm1_g[i]), bf(ffn1_wg[i]), bf(ffn1_wu[i]), bf(ffn1_wd[i]),
                 ctx=ctx if i == 0 else None)
        wp = bf(_gather_cols(w_in[i], in_cols))
        mla = (row2(mla_q_norm_g[i]), row2(mla_kv_norm_g[i]),
               bf(_gather_cols(mla_w_uq[i], mq)),
               bf(_gather_cols(mla_w_ukv[i], mk)), bf(_gather_cols(mla_w_ukv[i], mv)), cos_m, sin_m)
        swa_q, swa_kv, dn_qkv, dn_z, dn_ab, na_q, na_kv, mla_q, mla_k, mla_v = _inproj(
            h, mod[i, :, 1], row2(norm2_g[i]), wp, cos_a, sin_a, mla)

        ya = _swa(swa_q, swa_kv, swa_sink[i], band)
        yn = _na(na_q, na_kv, _na_bias_table(na_rpb[i] * LOG2E))
        ym = _mla_attn(mla_q, mla_k, mla_v, with_ctx=not last)
        pad = lambda v: jnp.zeros((1, LANES), F32).at[0, :2 * N_HEADS].set(v.reshape(-1))
        pk, egl = _dn_local(dn_qkv, dn_ab, dn_conv_w[i], pad(dn_a_log[i]), pad(dn_dt_bias[i]), bdm, expand, stm)
        o_f, o_b = _dn_scan(pk, egl, bdm)

        wo = bf(jnp.take(w_out[i], jnp.asarray(wo_rows), axis=0))
        mix = (ya, o_f, o_b, dn_z, ym, yn, mod[i, :, 1], row2(jnp.tile(dn_norm_g[i], N_HEADS)), bdm, wo)
        h = _ffn(h, mod[i, :, 2], row2(norm3_g[i]), bf(ffn2_wg[i]), bf(ffn2_wu[i]), bf(ffn2_wd[i]), mix=mix,
                 final_g=row2(final_norm_g) if last else None, latent_only=last)
    return h
```

```python
import functools

import numpy as np
import jax
import jax.numpy as jnp
from jax import lax
from jax.experimental import pallas as pl
from jax.experimental.pallas import tpu as pltpu

F32 = jnp.float32
BF16 = jnp.bfloat16
HIGHEST = lax.Precision.HIGHEST

D_MODEL = 1024
CTX_LEN = 256
GRID_W = 64
HEAD_DIM = 64
N_HEADS = 4
GROUP_W = N_HEADS * HEAD_DIM
D_FF = 2816
N_MOD = 9
ROPE_THETA = 10000.0
EPS = 1e-6
NEG_INF = -1e30
LOG2E = 1.4426950408889634
SWA_WINDOW = 128
SWA_BLOCK = 128
DN_CHUNK = 64
DN_LEVEL0 = 5
PK_U, PK_W, PK_QG, PK_KD, PK_IN = (slice(j * 256, (j + 1) * 256) for j in range(5))
PK_WIDTH = 5 * 256
MLA_Q_RANK = 256
MLA_KV_RANK = 128
MLA_NOPE = 64
MLA_ROPE = 32
MLA_V = 64
NA_KR = 8
NA_KC = 16
NA_UNITS = 4
IN_SIZES = (256, 128, 128, 768, 256, 8, 8, 256, 128, 32, 256, 256, 256)
IN_PROJ = sum(IN_SIZES)

LANES = 128
TOKEN_TILE = 256
VMEM_LIMIT = 56 * 1024 * 1024
MOD_ROWS = 16
CTX_ROW = 8
BATCH_ROWS = 2

P_SWA_Q, P_SWA_K, P_SWA_V = 0, 256, 384
P_DN_QKV, P_DN_Z, P_DN_AB = 512, 1280, 1536
P_MLA = 1664
P_NA_Q, P_NA_KV = 2176, 2432
P_TOTAL = 2944


def _cparams(sem):
    return pltpu.CompilerParams(dimension_semantics=sem, vmem_limit_bytes=VMEM_LIMIT)


def _dot(a, b, precision=None):
    return jnp.dot(a, b, preferred_element_type=F32, precision=precision)


def _dot_nt(a, b):
    return lax.dot_general(a, b, (((1,), (1,)), ((), ())), preferred_element_type=F32)


def _dot_tn(a, b):
    return lax.dot_general(a, b, (((0,), (0,)), ((), ())), preferred_element_type=F32)


def _sigmoid(x):
    return 1.0 / (1.0 + jnp.exp(-x))


def _silu(x):
    return x * _sigmoid(x)


def _resident(shape):
    nd = len(shape)
    return pl.BlockSpec(shape, lambda *_: (0,) * nd, pipeline_mode=pl.Buffered(1))


def _mod_kernel(c_ref, w_ref, b_ref, o_ref):
    s = _silu(c_ref[...])
    o_ref[0] = _dot(s, w_ref[0], HIGHEST) + b_ref[0]


def _modulation(cvec, ada_w, ada_b):
    n_layers = ada_w.shape[0]
    d = D_MODEL
    return pl.pallas_call(
        _mod_kernel,
        grid=(n_layers, N_MOD),
        in_specs=[
            pl.BlockSpec((MOD_ROWS, d), lambda l, j: (0, 0)),
            pl.BlockSpec((1, d, d), lambda l, j: (l, 0, j)),
            pl.BlockSpec((1, 1, d), lambda l, j: (l, 0, j)),
        ],
        out_specs=pl.BlockSpec((1, MOD_ROWS, d), lambda l, j: (l, 0, j)),
        out_shape=jax.ShapeDtypeStruct((n_layers, MOD_ROWS, N_MOD * d), F32),
        compiler_params=_cparams(("arbitrary", "arbitrary")),
        name="modulation",
    )(cvec, ada_w, ada_b.reshape(n_layers, 1, N_MOD * d))


def _mod_index(b, i, nb=1):
    return (jnp.where(i == 0, CTX_ROW // nb, b), 0, 0)


def _prenorm(x, g, shift, scale):
    ms = jnp.mean(x * x, axis=-1, keepdims=True)
    return (x * lax.rsqrt(ms + EPS) * g) * (1.0 + scale) + shift


def _group_mean_sq(x, ones_bd):
    return _dot((x * x).astype(BF16), ones_bd) * (1.0 / HEAD_DIM)


def _ffn_kernel(*refs, mix, final, split):
    it = iter(refs)
    x_ref = next(it)
    if split:
        ctx_ref = next(it)
    if mix:
        ya_ref, of_ref, ob_ref, z_ref, ym_ref, yn_ref = (next(it) for _ in range(6))
        ymc_ref = next(it) if mix == "ctx" else None
        mmod_ref, dng_ref, ones_ref, wo_ref = (next(it) for _ in range(4))
    mod_ref, g_ref, wg_ref, wu_ref, wd_ref = (next(it) for _ in range(5))
    if final:
        fg_ref = next(it)
    o_ref = next(it)

    nb, tm, _ = x_ref.shape
    xs, hs = [], []
    for bi in range(nb):
        x = x_ref[bi]
        if split:
            x = jnp.where(pl.program_id(1) == 0, ctx_ref[bi], x)
        if mix:
            o = of_ref[bi] + ob_ref[bi]
            yd = (o * lax.rsqrt(_group_mean_sq(o, ones_ref[...]) + EPS) * dng_ref[...]) * _silu(z_ref[bi])
            ym = ym_ref[bi] if ymc_ref is None else jnp.where(pl.program_id(1) == 0, ymc_ref[bi], ym_ref[bi])
            ycat = jnp.concatenate([ya_ref[bi], yd.astype(BF16), ym, yn_ref[bi]], axis=-1)
            x = x + mmod_ref[bi][2:3] * _dot(ycat, wo_ref[...])
        mod = mod_ref[bi]
        xs.append(x)
        hs.append(_prenorm(x, g_ref[...], mod[0:1], mod[1:2]).astype(BF16))
    h = jnp.concatenate(hs, axis=0)
    a = (_silu(_dot(h, wg_ref[...])) * _dot(h, wu_ref[...])).astype(BF16)
    y = _dot(a, wd_ref[...])
    for bi in range(nb):
        x = xs[bi] + (0.5 * mod_ref[bi][2:3]) * y[bi * tm:(bi + 1) * tm]
        if final:
            ms = jnp.mean(x * x, axis=-1, keepdims=True)
            x = x * lax.rsqrt(ms + EPS) * fg_ref[...]
        o_ref[bi] = x


def _ffn(x, mod, g, wg, wu, wd, mix=None, final_g=None, latent_only=False, ctx=None):
    bsz, n, d = x.shape
    tm = TOKEN_TILE
    nb = BATCH_ROWS if bsz % BATCH_ROWS == 0 else 1
    off = 1 if latent_only else 0
    tiles = n // tm - off + (ctx is not None)
    tok = lambda w: pl.BlockSpec((nb, tm, w), lambda b, i: (b, i + off, 0))
    modspec = pl.BlockSpec((nb, 3, d), lambda b, i: _mod_index(b, i + off, nb))
    if ctx is not None:
        assert mix is None and not latent_only and ctx.shape[1] == tm
        args = [x, ctx]
        specs = [pl.BlockSpec((nb, tm, d), lambda b, i: (b, jnp.maximum(i - 1, 0), 0)),
                 pl.BlockSpec((nb, tm, d), lambda b, i: (b, 0, 0))]
    else:
        args, specs = [x], [tok(d)]
    if mix is not None:
        ya, of, ob, z, (ym, ymc), yn, mmod, dng, ones_bd, wo = mix
        assert (ymc is None) == latent_only
        lat = pl.BlockSpec((nb, tm, ym.shape[-1]), lambda b, i: (b, jnp.maximum(i + off - 1, 0), 0))
        args += [ya, of, ob, z, ym, yn]
        specs += [tok(ya.shape[-1]), tok(of.shape[-1]), tok(ob.shape[-1]), tok(z.shape[-1]), lat, tok(yn.shape[-1])]
        if ymc is not None:
            args.append(ymc)
            specs.append(pl.BlockSpec((nb, tm, ymc.shape[-1]), lambda b, i: (b, 0, 0)))
        args += [mmod, dng, ones_bd, wo]
        specs += [modspec, _resident(dng.shape), _resident(ones_bd.shape), _resident(wo.shape)]
    args += [mod, g, wg, wu, wd]
    specs += [modspec, _resident(g.shape), _resident(wg.shape), _resident(wu.shape), _resident(wd.shape)]
    if final_g is not None:
        args.append(final_g)
        specs.append(_resident(final_g.shape))
    return pl.pallas_call(
        functools.partial(_ffn_kernel, mix=(None if mix is None else ("latent" if latent_only else "ctx")),
                          final=final_g is not None, split=ctx is not None),
        grid=(bsz // nb, tiles),
        in_specs=specs,
        out_specs=pl.BlockSpec((nb, tm, d), lambda b, i: (b, i, 0)),
        out_shape=jax.ShapeDtypeStruct((bsz, tiles * tm, d), F32),
        compiler_params=_cparams(("parallel", "arbitrary")),
        name="ffn_mix" if mix is not None else "ffn",
    )(*args)


def _rope_partner(x, lane, q):
    w = x.shape[-1]
    return jnp.where((lane % (2 * q)) < q, pltpu.roll(x, w - q, 1), pltpu.roll(x, q, 1))


def _inproj_kernel(x_ref, mod_ref, g_ref, w_ref, cos_ref, sin_ref,
                   gq_ref, gkv_ref, wq_ref, wk_ref, wv_ref, vone_ref, cosm_ref, sinm_ref,
                   swaq_ref, swakv_ref, dnqkv_ref, dnz_ref, dnab_ref, naq_ref, nakv_ref,
                   mq_ref, mk_ref, mv_ref):
    nb, tm, _ = x_ref.shape
    g = g_ref[...]
    h = jnp.concatenate([_prenorm(x_ref[bi], g, mod_ref[bi][0:1], mod_ref[bi][1:2]).astype(BF16)
                         for bi in range(nb)], axis=0)
    u_all = _dot(h, w_ref[...])
    cos, sin = cos_ref[...], sin_ref[...]
    cos2 = jnp.concatenate([cos, cos], axis=-1)
    sin2 = jnp.concatenate([sin, sin], axis=-1)
    scale = HEAD_DIM ** -0.5 * LOG2E
    cosm, sinm = cosm_ref[...], sinm_ref[...]
    cos4 = jnp.concatenate([cosm] * N_HEADS, axis=-1)
    sin4 = jnp.concatenate([sinm] * N_HEADS, axis=-1)
    mscale = (MLA_NOPE + MLA_ROPE) ** -0.5 * LOG2E
    lane4 = lax.broadcasted_iota(jnp.int32, (1, N_HEADS * LANES), 1)
    lane2 = lane4[:, :2 * LANES]
    for bi in range(nb):
        u = u_all[bi * tm:(bi + 1) * tm]
        q = u[:, P_SWA_Q:P_SWA_Q + 256]
        swaq_ref[bi] = ((q * cos2 + _rope_partner(q, lane2, HEAD_DIM // 4) * sin2) * scale).astype(BF16)
        k = u[:, P_SWA_K:P_SWA_K + 128]
        k = k * cos + _rope_partner(k, lane2[:, :LANES], HEAD_DIM // 4) * sin
        swakv_ref[bi] = jnp.concatenate([k, u[:, P_SWA_V:P_SWA_V + 128]], axis=-1).astype(BF16)
        dnqkv_ref[bi] = u[:, P_DN_QKV:P_DN_QKV + 768]
        dnz_ref[bi] = u[:, P_DN_Z:P_DN_Z + 256]
        dnab_ref[bi] = u[:, P_DN_AB:P_DN_AB + 128]
        naq_ref[bi] = (u[:, P_NA_Q:P_NA_Q + 256] * scale).astype(BF16)
        nakv_ref[bi] = u[:, P_NA_KV:P_NA_KV + 512].astype(BF16)

        cq, ckv = u[:, P_MLA:P_MLA + 256], u[:, P_MLA + 256:P_MLA + 384]
        kra = u[:, P_MLA + 384:P_MLA + 512]
        krb = _rope_partner(kra, lane2[:, :LANES], MLA_ROPE // 4)
        nq = (cq * lax.rsqrt(jnp.mean(cq * cq, axis=-1, keepdims=True) + EPS) * gq_ref[...]).astype(BF16)
        nkv = (ckv * lax.rsqrt(jnp.mean(ckv * ckv, axis=-1, keepdims=True) + EPS) * gkv_ref[...]).astype(BF16)
        qf = _dot(nq, wq_ref[...])
        mq_ref[bi] = ((qf * cos4 + _rope_partner(qf, lane4, MLA_ROPE // 4) * sin4) * mscale).astype(BF16)
        kr = kra * cosm + krb * sinm
        mk_ref[bi] = (_dot(nkv, wk_ref[...]) + jnp.concatenate([kr] * N_HEADS, axis=-1)).astype(BF16)
        mv_ref[bi] = (_dot(nkv, wv_ref[...]) + vone_ref[...]).astype(BF16)


def _inproj(x, mod, g, w, cos, sin, mla):
    bsz, n, d = x.shape
    tm = TOKEN_TILE
    gq, gkv, wq, wk, wv, cosm, sinm = mla
    one_lane = np.where((np.arange(N_HEADS * LANES) // LANES) % 2 == 0, MLA_V, 0)
    vone = jnp.asarray((np.arange(N_HEADS * LANES) % LANES == one_lane).astype(np.float32)).reshape(1, -1)
    widths = (256, 256, 768, 256, 128, 256, 512)
    dtypes = (BF16, BF16, F32, F32, F32, BF16, BF16)
    nb = BATCH_ROWS if bsz % BATCH_ROWS == 0 else 1
    tok = lambda w_: pl.BlockSpec((nb, tm, w_), lambda b, i: (b, i, 0))
    tab = pl.BlockSpec((tm, LANES), lambda b, i: (i, 0))
    res = [gq, gkv, wq, wk, wv, vone]
    mw = N_HEADS * LANES
    return pl.pallas_call(
        _inproj_kernel,
        grid=(bsz // nb, n // tm),
        in_specs=[tok(d), pl.BlockSpec((nb, 3, d), lambda b, i: _mod_index(b, i, nb)), _resident(g.shape),
                  _resident(w.shape), tab, tab]
                 + [_resident(a.shape) for a in res] + [tab, tab],
        out_specs=[tok(w_) for w_ in widths] + [tok(mw)] * 3,
        out_shape=[jax.ShapeDtypeStruct((bsz, n, w_), dt) for w_, dt in zip(widths, dtypes)]
                  + [jax.ShapeDtypeStruct((bsz, n, mw), BF16)] * 3,
        compiler_params=_cparams(("parallel", "arbitrary")),
        name="inproj",
    )(x, mod, g, w, cos, sin, *res, cosm, sinm)


def _swa_kernel(sink_ref, q_ref, kvp_ref, kvc_ref, kvn_ref, kvx_ref, band_ref, o_ref):
    i = pl.program_id(1)
    nt = pl.num_programs(1)
    blk = SWA_BLOCK
    lane = lax.broadcasted_iota(jnp.int32, (1, LANES), 1)
    hsel = [lane < HEAD_DIM, lane >= HEAD_DIM]
    hmask = [m.astype(F32) for m in hsel]
    q = q_ref[0]
    cur = kvc_ref[0]
    kvx = kvx_ref[0]
    latent = i > 0
    blocks = [(kvp_ref[0], cur[:blk], cur[blk:]), (cur[:blk], cur[blk:], kvn_ref[0])]
    valid = [(latent & (i > 1), latent, latent), (latent, latent, latent & (i < nt - 1))]
    rblk = lax.broadcasted_iota(jnp.int32, (4 * blk, 1), 0) // blk
    sk = jnp.where(rblk == 0, sink_ref[0], jnp.where(rblk == 1, sink_ref[2],
                                                       jnp.where(rblk == 2, sink_ref[1], sink_ref[3]))) * LOG2E
    zero = jnp.zeros((1, LANES), F32)
    s, vcat = [], []
    for u in range(2):
        qu = q[u * blk:(u + 1) * blk]
        qs = jnp.concatenate([jnp.where(hsel[kh], qu[:, g * LANES:(g + 1) * LANES], 0)
                              for g in range(2) for kh in range(2)], axis=0)
        kvs = list(blocks[u]) + [kvx]
        kcat = jnp.concatenate([t[:, :LANES] for t in kvs], axis=0)
        vcat.append(jnp.concatenate([t[:, LANES:] for t in kvs], axis=0))
        pen = jnp.concatenate([zero + jnp.where(ok, 0.0, NEG_INF) for ok in valid[u]] + [zero, zero], axis=-1)
        s.append(_dot_nt(qs, kcat) + band_ref[...] + pen)
    p, l = [], []
    for u in range(2):
        m = jnp.maximum(jnp.max(s[u], axis=-1, keepdims=True), sk)
        pu = jnp.exp2(s[u] - m)
        l.append(jnp.sum(pu, axis=-1, keepdims=True) + jnp.exp2(sk - m))
        p.append(pu.astype(BF16))
    for u in range(2):
        o = _dot(p[u], vcat[u]) * (1.0 / l[u])
        outs = [o[(2 * g) * blk:(2 * g + 1) * blk] * hmask[0] + o[(2 * g + 1) * blk:(2 * g + 2) * blk] * hmask[1]
                for g in range(2)]
        o_ref[0, u * blk:(u + 1) * blk, :] = jnp.concatenate(outs, axis=-1).astype(o_ref.dtype)


def _swa(q, kv, sink, band):
    bsz, n, _ = q.shape
    blk = SWA_BLOCK
    tm = 2 * blk
    first = CTX_LEN // blk
    last = n // blk - 1
    return pl.pallas_call(
        _swa_kernel,
        grid=(bsz, n // tm),
        in_specs=[
            pl.BlockSpec(memory_space=pltpu.SMEM),
            pl.BlockSpec((1, tm, 256), lambda b, i: (b, i, 0)),
            pl.BlockSpec((1, blk, 256), lambda b, i: (b, jnp.clip(2 * i - 1, first, last), 0)),
            pl.BlockSpec((1, tm, 256), lambda b, i: (b, jnp.maximum(i, 1), 0)),
            pl.BlockSpec((1, blk, 256), lambda b, i: (b, jnp.clip(2 * i + 2, first, last), 0)),
            pl.BlockSpec((1, CTX_LEN, 256), lambda b, i: (b, 0, 0)),
            _resident(band.shape),
        ],
        out_specs=pl.BlockSpec((1, tm, 256), lambda b, i: (b, i, 0)),
        out_shape=jax.ShapeDtypeStruct((bsz, n, 256), BF16),
        compiler_params=_cparams(("parallel", "arbitrary")),
        name="swa",
    )(sink, q, kv, kv, kv, kv, band)


def _na_kernel(q_ref, kv_ref, tab_ref, o_ref, *, rows):
    i = pl.program_id(1)
    w = GRID_W
    nctx = CTX_LEN // w
    nloc = NA_KR * w
    lane = lax.broadcasted_iota(jnp.int32, (1, GROUP_W), 1) // HEAD_DIM
    hsel = [lane == h for h in range(N_HEADS)]
    hmask = [m.astype(F32) for m in hsel]
    kv_ctx = kv_ref[0, pl.ds(0, CTX_LEN), :]
    s, vcat = [], []
    for u in range(NA_UNITS):
        j = NA_UNITS * i + u
        r = jnp.maximum(j - nctx, 0)
        rs = jnp.clip(r - NA_KR // 2, 0, rows - NA_KR)
        tab_i = jnp.where(j < nctx, NA_KR, r - rs)
        qu = q_ref[0, u * w:(u + 1) * w, :]
        qs = jnp.concatenate([jnp.where(hsel[h], qu, 0) for h in range(N_HEADS)], axis=0)
        start = pl.multiple_of(CTX_LEN + rs * w, w)
        kv_loc = kv_ref[0, pl.ds(start, nloc), :]
        kcat = jnp.concatenate([kv_loc[:, :GROUP_W], kv_ctx[:, :GROUP_W]], axis=0)
        vcat.append(jnp.concatenate([kv_loc[:, GROUP_W:], kv_ctx[:, GROUP_W:]], axis=0))
        su = _dot_nt(qs, kcat)
        s.append(jnp.concatenate([su[:, :nloc] + tab_ref[tab_i], su[:, nloc:]], axis=-1))
    p, l = [], []
    for u in range(NA_UNITS):
        pu = jnp.exp2(s[u] - jnp.max(s[u], axis=-1, keepdims=True))
        l.append(jnp.sum(pu, axis=-1, keepdims=True))
        p.append(pu.astype(BF16))
    for u in range(NA_UNITS):
        o = _dot(p[u], vcat[u]) * (1.0 / l[u])
        acc = o[0:w] * hmask[0]
        for h in range(1, N_HEADS):
            acc = acc + o[h * w:(h + 1) * w] * hmask[h]
        o_ref[0, u * w:(u + 1) * w, :] = acc.astype(o_ref.dtype)


def _na(q, kv, tab):
    bsz, n, _ = q.shape
    w = GRID_W
    rows = (n - CTX_LEN) // w
    assert rows >= NA_KR
    return pl.pallas_call(
        functools.partial(_na_kernel, rows=rows),
        grid=(bsz, n // (NA_UNITS * w)),
        in_specs=[
            pl.BlockSpec((1, NA_UNITS * w, 256), lambda b, i: (b, i, 0)),
            pl.BlockSpec((1, n, 512), lambda b, i: (b, 0, 0)),
            _resident(tab.shape),
        ],
        out_specs=pl.BlockSpec((1, NA_UNITS * w, 256), lambda b, i: (b, i, 0)),
        out_shape=jax.ShapeDtypeStruct((bsz, n, 256), BF16),
        compiler_params=_cparams(("parallel", "arbitrary")),
        name="na",
    )(q, kv, tab)


def _mla_attn_kernel(*refs, nkeys):
    *q_refs, k_ref, v_ref, o_ref = refs
    tq = TOKEN_TILE
    units = [(r, h) for r in range(len(q_refs)) for h in range(N_HEADS)]
    hl = lambda h: slice(h * LANES, (h + 1) * LANES)
    low = lax.broadcasted_iota(jnp.int32, (1, LANES), 1) < MLA_V
    s, p, even = {}, {}, {}
    for t in range(len(units) + 2):
        if t < len(units):
            r, h = units[t]
            s[t] = _dot_nt(q_refs[r][0, :, hl(h)], k_ref[0, pl.ds(0, nkeys), hl(h)])
        if 0 <= t - 1 < len(units):
            sh = s.pop(t - 1)
            p[t - 1] = jnp.exp2(sh - jnp.max(sh, axis=-1, keepdims=True)).astype(BF16)
        if 0 <= t - 2 < len(units):
            r, h = units[t - 2]
            o = _dot(p.pop(t - 2), v_ref[0, pl.ds(0, nkeys), hl(h)])
            if h % 2 == 0:
                even[r] = o * (1.0 / o[:, MLA_V:MLA_V + 1])
            else:
                pair = jnp.where(low, even.pop(r), o * (1.0 / o[:, 0:1]))
                o_ref[0, r * tq:(r + 1) * tq, hl(h // 2)] = pair.astype(o_ref.dtype)


def _mla_attn(q, k, v, with_ctx):
    bsz, n, w = q.shape
    wo = GROUP_W
    tq = TOKEN_TILE
    pairs = (n // tq - 1) // 2
    assert 2 * pairs + 1 == n // tq
    whole = pl.BlockSpec((1, n, w), lambda b, i: (b, 0, 0))
    lat = pl.pallas_call(
        functools.partial(_mla_attn_kernel, nkeys=n),
        grid=(bsz, pairs),
        in_specs=[pl.BlockSpec((1, tq, w), lambda b, i: (b, 2 * i + 1, 0)),
                  pl.BlockSpec((1, tq, w), lambda b, i: (b, 2 * i + 2, 0)), whole, whole],
        out_specs=pl.BlockSpec((1, 2 * tq, wo), lambda b, i: (b, i, 0)),
        out_shape=jax.ShapeDtypeStruct((bsz, n - tq, wo), BF16),
        compiler_params=_cparams(("parallel", "arbitrary")),
        name="mla_attn",
    )(q, q, k, v)
    if not with_ctx:
        return lat, None
    first = pl.BlockSpec((1, tq, w), lambda b: (b, 0, 0))
    ctx = pl.pallas_call(
        functools.partial(_mla_attn_kernel, nkeys=CTX_LEN),
        grid=(bsz,),
        in_specs=[first, first, first],
        out_specs=pl.BlockSpec((1, tq, wo), lambda b: (b, 0, 0)),
        out_shape=jax.ShapeDtypeStruct((bsz, tq, wo), BF16),
        compiler_params=_cparams(("parallel",)),
        name="mla_attn_ctx",
    )(q, k, v)
    return lat, ctx


def _bd(m, bd_mask):
    return jnp.concatenate([m] * N_HEADS, axis=0) * bd_mask


def _dn_local_kernel(x_ref, xp_ref, xn_ref, ab_ref, cw_ref, alog_ref, dtb_ref, bdm_ref, exp_ref, stm_ref,
                     pk_ref, egl_ref):
    i = pl.program_id(1)
    nt = pl.num_programs(1)
    tm = TOKEN_TILE
    c = DN_CHUNK
    nch = tm // c
    nb = x_ref.shape[0]
    row = lax.broadcasted_iota(jnp.int32, (tm, 1), 0)
    lane = lax.broadcasted_iota(jnp.int32, (1, LANES), 1)
    bdm = bdm_ref[...]
    fronts = [_dn_front(x_ref[bi], xp_ref[bi][7:8], xn_ref[bi][0:1], ab_ref[bi], cw_ref[...], alog_ref[...],
                        dtb_ref[...], bdm, exp_ref[...], row, lane, i, nt) for bi in range(nb)]

    eye = stm_ref[4]
    tile_b = eye.astype(BF16)
    chains = [(bi, d, ci) for bi in range(nb) for ci in range(nch) for d in range(2)]
    rows = lambda ci: slice(ci * c, (ci + 1) * c)
    kq = {}
    for bi in range(nb):
        q, k, _, _, bexps = fronts[bi]
        for ci in range(nch):
            kc = k[rows(ci)]
            kt_bd = _dot_tn(kc.astype(BF16), tile_b).astype(BF16) * bdm
            lhs = jnp.concatenate([kc * bexps[0][rows(ci)], kc * bexps[1][rows(ci)], q[rows(ci)]], axis=0)
            kq[bi, ci] = _dot(lhs.astype(BF16), kt_bd)
    lbs, xinvs = {}, {}
    for bi, d, ci in chains:
        q, k, _, gcols, _ = fronts[bi]
        sl = rows(ci)
        gcol = gcols[d][sl]
        grow = jnp.sum(gcol * eye, axis=0, keepdims=True)
        incl, strict = stm_ref[2 * d], stm_ref[2 * d + 1]
        dec_incl = jnp.exp((gcol - grow) * incl) * incl
        lmat = kq[bi, ci][d * c:(d + 1) * c] * (dec_incl * strict)
        egc = jnp.exp(gcol)
        glast = gcol[c - 1:c] if d == 0 else gcol[0:1]
        pk_ref[d, bi, sl, PK_IN] = (kq[bi, ci][2 * c:3 * c] * dec_incl).astype(BF16)
        pk_ref[d, bi, sl, PK_QG] = (q[sl] * egc).astype(BF16)
        pk_ref[d, bi, sl, PK_KD] = (k[sl] * jnp.exp(glast - gcol)).astype(BF16)
        egl_ref[d, bi, ci * 8:(ci + 1) * 8, :] = jnp.broadcast_to(jnp.exp(glast), (8, GROUP_W))
        lbs[bi, d, ci] = lmat.astype(BF16)
        xinvs[bi, d, ci] = eye - lmat * stm_ref[DN_LEVEL0 + 6 * d]
    for lvl in range(1, 6):
        ys = {}
        for ch in chains:
            off = _bd(lbs[ch] * stm_ref[DN_LEVEL0 + 6 * ch[1] + lvl].astype(BF16), bdm)
            ys[ch] = _dot(xinvs[ch].astype(BF16), off)
        for ch in chains:
            xinvs[ch] = xinvs[ch] - _dot(ys[ch].astype(BF16), _bd(xinvs[ch].astype(BF16), bdm))
    for bi, d, ci in chains:
        _, k, v, gcols, bexps = fronts[bi]
        sl = rows(ci)
        xb = xinvs[bi, d, ci].astype(BF16)
        kb = k[sl] * bexps[d][sl]
        pk_ref[d, bi, sl, PK_U] = _dot(xb, _bd((v[sl] * bexps[d][sl]).astype(BF16), bdm)).astype(BF16)
        pk_ref[d, bi, sl, PK_W] = _dot(xb, _bd((kb * jnp.exp(gcols[d][sl])).astype(BF16), bdm)).astype(BF16)


def _dn_front(x, prev_row, next_row, ab, cw, alog, dtb, bdm, expand, row, lane, i, nt):
    tm = TOKEN_TILE
    c = DN_CHUNK
    prev_row = prev_row * (i > 1).astype(F32)
    next_row = next_row * ((i > 0) & (i < nt - 1)).astype(F32)
    xm1 = jnp.where(row == 0, prev_row, pltpu.roll(x, 1, 0))
    xp1 = jnp.where(row == tm - 1, next_row, pltpu.roll(x, tm - 1, 0))
    h = _silu(cw[0:1] * xm1 + cw[1:2] * x + cw[2:3] * xp1)
    q, k, v = h[:, 0:256], h[:, 256:512], h[:, 512:768]
    q = q * lax.rsqrt(_dot((q * q).astype(BF16), bdm) + EPS) * (HEAD_DIM ** -0.5)
    k = k * lax.rsqrt(_dot((k * k).astype(BF16), bdm) + EPS)

    z = ab + dtb
    softplus = jnp.maximum(z, 0.0) + jnp.log(1.0 + jnp.exp(-jnp.abs(z)))
    t = jnp.where(lane < 2 * N_HEADS, -jnp.exp(alog) * softplus, _sigmoid(ab))
    t = jnp.where(lane < 4 * N_HEADS, t, 0.0)
    rowc = row % c
    pre, suf = t, t
    step = 1
    while step < c:
        pre = pre + jnp.where(rowc >= step, pltpu.roll(pre, step, 0), 0.0)
        suf = suf + jnp.where(rowc < c - step, pltpu.roll(suf, tm - step, 0), 0.0)
        step *= 2
    t = jnp.where(lane < N_HEADS, pre, jnp.where(lane < 2 * N_HEADS, suf, t))
    hi = t.astype(BF16).astype(F32)
    rem = t - hi
    mid = rem.astype(BF16).astype(F32)
    comb = (hi + pltpu.roll(mid, 4 * N_HEADS, 1) + pltpu.roll(rem - mid, 8 * N_HEADS, 1)).astype(BF16)
    ex = _dot(comb, expand)
    return q, k, v, [ex[:, 0:256], ex[:, 256:512]], [ex[:, 512:768], ex[:, 768:1024]]


def _dn_local(qkv, ab, cw, alog, dtb, bdm, expand, stm):
    bsz, n, _ = qkv.shape
    tm = TOKEN_TILE
    nt = n // tm
    r8 = tm // 8
    nb = next(k for k in (4, 2, 1) if bsz % k == 0)
    tok = lambda w_: pl.BlockSpec((nb, tm, w_), lambda b, i: (b, i, 0))
    outs = [jax.ShapeDtypeStruct((2, bsz, n, PK_WIDTH), BF16), jax.ShapeDtypeStruct((2, bsz, n // 8, 256), F32)]
    return pl.pallas_call(
        _dn_local_kernel,
        grid=(bsz // nb, nt),
        in_specs=[
            tok(768),
            pl.BlockSpec((nb, 8, 768), lambda b, i: (b, jnp.maximum(i * r8 - 1, 0), 0)),
            pl.BlockSpec((nb, 8, 768), lambda b, i: (b, jnp.minimum((i + 1) * r8, nt * r8 - 1), 0)),
            tok(128),
        ] + [_resident(a.shape) for a in (cw, alog, dtb, bdm, expand, stm)],
        out_specs=[pl.BlockSpec((2, nb, tm, PK_WIDTH), lambda b, i: (0, b, i, 0)),
                   pl.BlockSpec((2, nb, tm // 8, 256), lambda b, i: (0, b, i, 0))],
        out_shape=outs,
        compiler_params=_cparams(("parallel", "arbitrary")),
        name="dn_local",
    )(qkv, qkv, qkv, ab, cw, alog, dtb, bdm, expand, stm)


def _dn_scan_kernel(pf_ref, ef_ref, pb_ref, eb_ref, bdm_ref, of_ref, ob_ref, s_ref):
    c = DN_CHUNK
    nch = TOKEN_TILE // c
    nb = of_ref.shape[0]

    @pl.when(pl.program_id(1) == 0)
    def _():
        s_ref[...] = jnp.zeros_like(s_ref)

    bdm = bdm_ref[...]
    bdm_f = bdm.astype(F32)
    ins = ((pf_ref, ef_ref), (pb_ref, eb_ref))
    outs = (of_ref, ob_ref)
    for step in range(nch):
        chains = [(d, bi, step if d == 0 else nch - 1 - step) for bi in range(nb) for d in range(2)]
        rows = lambda ci: slice(ci * c, (ci + 1) * c)
        ws, vb = {}, {}
        for d, bi, ci in chains:
            p_ref = ins[d][0]
            lhs = jnp.concatenate([p_ref[0, bi, rows(ci), PK_W], p_ref[0, bi, rows(ci), PK_QG]], axis=0)
            ws[d, bi] = _dot(lhs, s_ref[d, bi].astype(BF16))
        for d, bi, ci in chains:
            p_ref = ins[d][0]
            vb[d, bi] = (p_ref[0, bi, rows(ci), PK_U].astype(F32) - ws[d, bi][:c]).astype(BF16)
            outs[d][bi, rows(ci), :] = ws[d, bi][c:] + _dot(p_ref[0, bi, rows(ci), PK_IN], _bd(vb[d, bi], bdm))
        for d, bi, ci in chains:
            p_ref, e_ref = ins[d]
            egl = e_ref[0, bi, ci * 8:ci * 8 + 1, :]
            s_ref[d, bi] = s_ref[d, bi] * egl + _dot_tn(p_ref[0, bi, rows(ci), PK_KD], vb[d, bi]) * bdm_f


def _dn_scan(pk, egl, bdm):
    _, bsz, n, _ = pk.shape
    tm = TOKEN_TILE
    nt = n // tm
    nb = next(k for k in (8, 4, 2, 1) if bsz % k == 0)
    bidx = lambda s: jnp.where(s == 0, 0, nt - s)
    f4 = lambda r, w_: pl.BlockSpec((1, nb, r, w_), lambda b, s: (0, b, s, 0))
    b4 = lambda r, w_: pl.BlockSpec((1, nb, r, w_), lambda b, s: (1, b, bidx(s), 0))
    return pl.pallas_call(
        _dn_scan_kernel,
        grid=(bsz // nb, nt),
        in_specs=[f4(tm, PK_WIDTH), f4(tm // 8, 256), b4(tm, PK_WIDTH), b4(tm // 8, 256), _resident(bdm.shape)],
        out_specs=[pl.BlockSpec((nb, tm, 256), lambda b, s: (b, s, 0)),
                   pl.BlockSpec((nb, tm, 256), lambda b, s: (b, bidx(s), 0))],
        out_shape=[jax.ShapeDtypeStruct((bsz, n, 256), F32)] * 2,
        scratch_shapes=[pltpu.VMEM((2, nb, GROUP_W, GROUP_W), F32)],
        compiler_params=_cparams(("parallel", "arbitrary")),
        name="dn_scan",
    )(pk, egl, pk, egl, bdm)


def _inproj_columns():
    off = np.cumsum((0,) + IN_SIZES)
    z = IN_PROJ
    gg, kk, dd = np.meshgrid(np.arange(2), np.arange(2), np.arange(64), indexing="ij")
    swa_q = (off[0] + kk * 128 + gg * 64 + dd).reshape(-1)
    zeros = lambda n: np.full((n,), z)
    cols = np.concatenate([
        swa_q, off[1] + np.arange(128), off[2] + np.arange(128),
        off[3] + np.arange(768), off[4] + np.arange(256),
        off[5] + np.arange(8), off[6] + np.arange(8), zeros(112),
        off[7] + np.arange(256), off[8] + np.arange(128),
        zeros(64), off[9] + np.arange(32), zeros(32),
        off[10] + np.arange(256), off[11] + np.arange(256), off[12] + np.arange(256),
    ])
    assert cols.shape == (P_TOTAL,)
    return cols


def _gather_cols(w, cols):
    wz = jnp.concatenate([w, jnp.zeros(w.shape[:-1] + (1,), w.dtype)], axis=-1)
    return jnp.take(wz, jnp.asarray(cols), axis=-1)


def _mla_columns():
    dqk = MLA_NOPE + MLA_ROPE
    zq, zkv = N_HEADS * dqk, N_HEADS * (MLA_NOPE + MLA_V)
    wq, wk, wv = [], [], []
    for h in range(N_HEADS):
        wq += [h * dqk + np.arange(dqk), np.full((32,), zq)]
        wk += [h * 128 + np.arange(64), np.full((64,), zkv)]
        vcols, pad = h * 128 + 64 + np.arange(64), np.full((64,), zkv)
        wv += [vcols, pad] if h % 2 == 0 else [pad, vcols]
    return tuple(np.concatenate(c) for c in (wq, wk, wv))


def _wout_rows():
    gg, kk, dd = np.meshgrid(np.arange(2), np.arange(2), np.arange(64), indexing="ij")
    ya = (kk * 128 + gg * 64 + dd).reshape(-1)
    return np.concatenate([ya, GROUP_W + np.arange(3 * GROUP_W)])


def _rope_tables(seq):
    t = np.arange(seq)
    row = (t // GRID_W).astype(np.float32)
    col = (t % GRID_W).astype(np.float32)

    def cs(n):
        inv = (np.float32(ROPE_THETA) ** (-np.arange(0, n, 2, dtype=np.float32) / np.float32(n))).astype(np.float32)
        ar, ac = row[:, None] * inv[None, :], col[:, None] * inv[None, :]
        cos = np.concatenate([np.cos(ar)] * 2 + [np.cos(ac)] * 2, axis=-1)
        sin = np.concatenate([-np.sin(ar), np.sin(ar), -np.sin(ac), np.sin(ac)], axis=-1)
        return cos.astype(np.float32), sin.astype(np.float32)

    cos64, sin64 = cs(HEAD_DIM // 2)
    cos32, sin32 = cs(MLA_ROPE // 2)
    ones = lambda w_: np.ones((CTX_LEN, w_), np.float32)
    zeros = lambda w_: np.zeros((CTX_LEN, w_), np.float32)
    cos_a = np.concatenate([ones(128), np.concatenate([cos64, cos64], axis=-1)], axis=0)
    sin_a = np.concatenate([zeros(128), np.concatenate([sin64, sin64], axis=-1)], axis=0)
    pad1, pad0 = np.ones((seq, 64), np.float32), np.zeros((seq, 32), np.float32)
    cos_m = np.concatenate([np.concatenate([ones(96), zeros(32)], axis=-1),
                            np.concatenate([pad1, cos32, pad0], axis=-1)], axis=0)
    sin_m = np.concatenate([zeros(128),
                            np.concatenate([np.zeros((seq, 64), np.float32), sin32, pad0], axis=-1)], axis=0)
    return tuple(jnp.asarray(a) for a in (cos_a, sin_a, cos_m, sin_m))


def _swa_band():
    i = np.arange(4 * SWA_BLOCK)[:, None] % SWA_BLOCK
    j = np.arange(3 * SWA_BLOCK + CTX_LEN)[None, :]
    ok = (j >= 3 * SWA_BLOCK) | ((j >= i) & (j <= i + 2 * SWA_WINDOW))
    return jnp.asarray(np.where(ok, 0.0, NEG_INF), F32)


def _na_bias_table(rpb):
    w = GRID_W
    qc = np.arange(w)[:, None]
    kc = np.arange(w)[None, :]
    cs = np.clip(qc - NA_KC // 2, 0, w - NA_KC)
    ok = (kc >= cs) & (kc < cs + NA_KC)
    dc = np.clip(kc - qc + NA_KC - 1, 0, 2 * NA_KC - 2)
    onehot = (np.arange(2 * NA_KC - 1)[None, :, None] == dc[:, None, :]).astype(np.float32)
    full = jnp.einsum("hrd,qdk->hrqk", rpb, jnp.asarray(onehot), precision=HIGHEST)
    full = jnp.where(jnp.asarray(ok)[None, None], full, NEG_INF)
    tabs = [jnp.transpose(full[:, NA_KR - 1 - dl:2 * NA_KR - 1 - dl], (0, 2, 1, 3)).reshape(N_HEADS * w, NA_KR * w)
            for dl in range(NA_KR)]
    tabs.append(jnp.full((N_HEADS * w, NA_KR * w), NEG_INF, F32))
    return jnp.stack(tabs)


def _dn_constants():
    c = DN_CHUNK
    hh = np.arange(GROUP_W) // HEAD_DIM
    bd = (hh[:, None] == hh[None, :])
    expand = np.zeros((LANES, 4 * GROUP_W), np.float32)
    for piece in range(3):
        for d in range(2):
            expand[16 * piece + d * 4 + hh, d * GROUP_W + np.arange(GROUP_W)] = 1.0
            expand[16 * piece + 8 + d * 4 + hh, (2 + d) * GROUP_W + np.arange(GROUP_W)] = 1.0
    i = np.arange(c)[:, None]
    jj = (np.arange(GROUP_W) % c)[None, :]
    masks = [(jj <= i), (jj < i), (jj >= i), (jj > i), (jj == i)]
    for d in range(2):
        for lvl in range(6):
            b = 1 << lvl
            same = (i // (2 * b)) == (jj // (2 * b))
            lo, hi = (jj % (2 * b)) < b, (i % (2 * b)) >= b
            masks.append(same & (lo & hi if d == 0 else ~lo & ~hi))
    stm = np.stack(masks).astype(np.float32)
    return jnp.asarray(bd, BF16), jnp.asarray(expand, BF16), jnp.asarray(stm)


def kernel(x, c, ctx, c_ctx, ada_w, ada_b, norm1_g, ffn1_wg, ffn1_wu, ffn1_wd, norm2_g, w_in, swa_sink,
           dn_conv_w, dn_a_log, dn_dt_bias, dn_norm_g, mla_q_norm_g, mla_w_uq, mla_kv_norm_g, mla_w_ukv,
           na_rpb, w_out, norm3_g, ffn2_wg, ffn2_wu, ffn2_wd, final_norm_g):
    bsz, seq, d = x.shape
    depth = ada_w.shape[0]
    assert d == D_MODEL and ctx.shape[1] == CTX_LEN and seq % TOKEN_TILE == 0 and bsz <= CTX_ROW
    assert CTX_ROW % BATCH_ROWS == 0 and CTX_ROW + BATCH_ROWS <= MOD_ROWS

    cvec = jnp.zeros((MOD_ROWS, d), F32).at[:bsz].set(c).at[CTX_ROW:].set(c_ctx)
    mod = _modulation(cvec, ada_w, ada_b).reshape(depth, MOD_ROWS, 3, 3, d)

    cos_a, sin_a, cos_m, sin_m = _rope_tables(seq)
    band = _swa_band()
    bdm, expand, stm = _dn_constants()
    in_cols = _inproj_columns()
    mq, mk, mv = _mla_columns()
    wo_rows = _wout_rows()
    row2 = lambda v: v.reshape(1, -1)

    h = x
    for i in range(depth):
        last = i == depth - 1
        bf = lambda a: a.astype(BF16)
        h = _ffn(h, mod[i, :, 0], row2(norm1_g[i]), bf(ffn1_wg[i]), bf(ffn1_wu[i]), bf(ffn1_wd[i]),
                 ctx=ctx if i == 0 else None)
        wp = bf(_gather_cols(w_in[i], in_cols))
        mla = (row2(mla_q_norm_g[i]), row2(mla_kv_norm_g[i]),
               bf(_gather_cols(mla_w_uq[i], mq)),
               bf(_gather_cols(mla_w_ukv[i], mk)), bf(_gather_cols(mla_w_ukv[i], mv)), cos_m, sin_m)
        swa_q, swa_kv, dn_qkv, dn_z, dn_ab, na_q, na_kv, mla_q, mla_k, mla_v = _inproj(
            h, mod[i, :, 1], row2(norm2_g[i]), wp, cos_a, sin_a, mla)

        ya = _swa(swa_q, swa_kv, swa_sink[i], band)
        yn = _na(na_q, na_kv, _na_bias_table(na_rpb[i] * LOG2E))
        ym = _mla_attn(mla_q, mla_k, mla_v, with_ctx=not last)
        pad = lambda v: jnp.zeros((1, LANES), F32).at[0, :2 * N_HEADS].set(v.reshape(-1))
        pk, egl = _dn_local(dn_qkv, dn_ab, dn_conv_w[i], pad(dn_a_log[i]), pad(dn_dt_bias[i]), bdm, expand, stm)
        o_f, o_b = _dn_scan(pk, egl, bdm)

        wo = bf(jnp.take(w_out[i], jnp.asarray(wo_rows), axis=0))
        mix = (ya, o_f, o_b, dn_z, ym, yn, mod[i, :, 1], row2(jnp.tile(dn_norm_g[i], N_HEADS)), bdm, wo)
        h = _ffn(h, mod[i, :, 2], row2(norm3_g[i]), bf(ffn2_wg[i]), bf(ffn2_wu[i]), bf(ffn2_wd[i]), mix=mix,
                 final_g=row2(final_norm_g) if last else None, latent_only=last)
    return h
```
